```python
import math
import jax
import jax.numpy as jnp
from jax import lax
import numpy as np

D_MODEL = 1024
BATCH = 8
SEQ = 2048
DEPTH = 2

GRID_W = 64
CTX_LEN = 256
Q_BLOCK = 128
ROPE_BASE = 10000.0
EPS = 1e-6
N_BRANCH = 4
A_HEADS = 4
A_QK_DIM = 64
A_V_DIM = 128
B_HEADS = 8
B_HEAD_DIM = 64
NA_KH = 8
NA_KW = 16
C_HEADS = 8
C_KV_HEADS = 2
C_HEAD_DIM = 64
D_HEADS = 8
MLA_Q_LORA = 256
MLA_KV_LORA = 128
MLA_NOPE = 64
MLA_ROPE = 32
MLA_V = 64
FFN_DENSE = 2816
N_EXPERTS = 8
TOP_K = 2
FFN_EXPERT = 3584

IN_WIDTHS = (
    A_HEADS * 2 * A_QK_DIM, A_HEADS * 2 * A_QK_DIM, A_HEADS * A_V_DIM,
    B_HEADS * B_HEAD_DIM, B_HEADS * B_HEAD_DIM, B_HEADS * B_HEAD_DIM,
    C_HEADS * C_HEAD_DIM, C_KV_HEADS * C_HEAD_DIM, C_KV_HEADS * C_HEAD_DIM,
    MLA_Q_LORA, MLA_KV_LORA, MLA_ROPE,
    N_BRANCH * D_MODEL,
)
D_IN = sum(IN_WIDTHS)

kernel_name = 'hybrid_gated_dit_trunk'


def split_in(p):
    cuts = [int(v) for v in np.cumsum(IN_WIDTHS)[:-1]]
    return jnp.split(p, cuts, axis=-1)


def rmsnorm(x, g):
    xf = x.astype(jnp.float32)
    y = xf * lax.rsqrt(jnp.mean(xf * xf, axis=-1, keepdims=True) + EPS)
    return (y * g.astype(jnp.float32)).astype(x.dtype)


def to_heads(z, n_heads):
    zb, zs, zw = z.shape
    return z.reshape(zb, zs, n_heads, zw // n_heads).transpose(0, 2, 1, 3)


def from_heads(z):
    zb, zh, zs, zd = z.shape
    return z.transpose(0, 2, 1, 3).reshape(zb, zs, zh * zd)


def axial_rope_tables(seq_len, d_rot):
    t = jnp.arange(seq_len)
    row = (t // GRID_W).astype(jnp.float32)
    col = (t % GRID_W).astype(jnp.float32)
    d_ax = d_rot // 2
    inv = ROPE_BASE ** (-jnp.arange(0, d_ax, 2, dtype=jnp.float32) / d_ax)
    ang_r = row[:, None] * inv[None]
    ang_c = col[:, None] * inv[None]
    ang = jnp.concatenate([ang_r, ang_r, ang_c, ang_c], axis=-1)
    return jnp.cos(ang), jnp.sin(ang)


def _rotate_half(z):
    h = z.shape[-1] // 2
    return jnp.concatenate([-z[..., h:], z[..., :h]], axis=-1)


def apply_rope(x, cos, sin):
    d_ax = x.shape[-1] // 2
    xf = x.astype(jnp.float32)
    rot = jnp.concatenate([_rotate_half(xf[..., :d_ax]), _rotate_half(xf[..., d_ax:])], axis=-1)
    return (xf * cos + rot * sin).astype(x.dtype)


def block_attention(q, k, v, coefs, scale):
    bq, hq, mq, sq, dk = q.shape
    nblk = sq // Q_BLOCK
    qb = jnp.moveaxis(q.reshape(bq, hq, mq, nblk, Q_BLOCK, dk), 3, 0)

    def one_block(qblk):
        s = jnp.einsum('bhmqd,bhmkd->bhmqk', qblk, k, preferred_element_type=jnp.float32) * scale
        p = jax.nn.softmax(s, axis=-1)
        w = jnp.einsum('bhmqk,m->bhqk', p, coefs)
        return jnp.einsum('bhqk,bhkd->bhqd', w.astype(v.dtype), v)

    out = lax.map(one_block, qb)
    return jnp.moveaxis(out, 0, 2).reshape(bq, hq, sq, v.shape[-1])


def neighbourhood_attention(q, k, v, k_ctx, v_ctx, rpb, rows):
    bq, hq, sq, d = q.shape
    kh = min(NA_KH, rows)
    kw = NA_KW
    scale = d ** -0.5
    c_len = k_ctx.shape[2]
    qg = q.reshape(bq, hq, rows, GRID_W, d)
    kg = k.reshape(bq, hq, rows, GRID_W, d)
    vg = v.reshape(bq, hq, rows, GRID_W, d)
    r = jnp.arange(rows)
    row_start = jnp.clip(r - kh // 2, 0, rows - kh)
    col = jnp.arange(GRID_W)
    col_start = jnp.clip(col - kw // 2, 0, GRID_W - kw)
    in_win = (col[None, :] >= col_start[:, None]) & (col[None, :] < col_start[:, None] + kw)
    dc = jnp.clip(col[None, :] - col[:, None], -(kw - 1), kw - 1) + (NA_KW - 1)

    def one_row(args):
        q_row, start, r_q = args
        k_rows = lax.dynamic_slice_in_dim(kg, start, kh, axis=2)
        v_rows = lax.dynamic_slice_in_dim(vg, start, kh, axis=2)
        dr = start + jnp.arange(kh) - r_q + (NA_KH - 1)
        bias = rpb[:, dr[None, :, None], dc[:, None, :]]
        s_win = jnp.einsum('bhqd,bhkwd->bhqkw', q_row, k_rows, preferred_element_type=jnp.float32) * scale
        s_win = s_win + bias[None].astype(jnp.float32)
        s_win = jnp.where(in_win[:, None, :], s_win, -jnp.inf).reshape(bq, hq, GRID_W, kh * GRID_W)
        s_ctx = jnp.einsum('bhqd,bhcd->bhqc', q_row, k_ctx, preferred_element_type=jnp.float32) * scale
        p = jax.nn.softmax(jnp.concatenate([s_ctx, s_win], axis=-1), axis=-1)
        p_ctx = p[..., :c_len].astype(v.dtype)
        p_win = p[..., c_len:].reshape(bq, hq, GRID_W, kh, GRID_W).astype(v.dtype)
        return (jnp.einsum('bhqc,bhcd->bhqd', p_ctx, v_ctx)
                + jnp.einsum('bhqkw,bhkwd->bhqd', p_win, v_rows))

    out = lax.map(one_row, (jnp.moveaxis(qg, 2, 0), row_start, r))
    return jnp.moveaxis(out, 0, 2).reshape(bq, hq, sq, d)


def hybrid_mixer(h, hc, want_ctx, rows, rope_a, rope_c, rope_d, lam_init,
                 w_in, lam_q1, lam_k1, lam_q2, lam_k2, g_diff_sub, na_rpb,
                 g_qnorm, g_knorm, g_q_lora, w_uq, g_kv_lora, w_ukv,
                 w_br_a, w_br_b, w_br_c, w_br_d, w_out):
    f32 = jnp.float32
    ones1 = jnp.ones((1,), f32)
    aq, ak, av, bq, bk, bv, cq, ck, cv, dqa, dkva, dkr, gates = split_in(jnp.einsum('bsd,de->bse', h, w_in))
    aq_c, ak_c, av_c, bq_c, bk_c, bv_c, cq_c, ck_c, cv_c, dqa_c, dkva_c, dkr_c, gates_c = split_in(
        jnp.einsum('bsd,de->bse', hc, w_in))

    def diff_qk(z):
        zb, zs, _ = z.shape
        return z.reshape(zb, zs, A_HEADS, 2, A_QK_DIM).transpose(0, 2, 3, 1, 4)

    lam = (jnp.exp(jnp.sum(lam_q1.astype(f32) * lam_k1.astype(f32)))
           - jnp.exp(jnp.sum(lam_q2.astype(f32) * lam_k2.astype(f32))) + lam_init)
    coefs = jnp.stack([jnp.ones_like(lam), -lam])
    scale_a = A_QK_DIM ** -0.5

    def diff_out(o):
        return from_heads(rmsnorm(o, g_diff_sub) * (1.0 - lam_init))

    qa = apply_rope(diff_qk(aq), *rope_a)
    ka = apply_rope(diff_qk(ak), *rope_a)
    va = to_heads(av, A_HEADS)
    ka_c = diff_qk(ak_c)
    va_c = to_heads(av_c, A_HEADS)
    oa = diff_out(block_attention(qa, jnp.concatenate([ka_c, ka], axis=3),
                                  jnp.concatenate([va_c, va], axis=2), coefs, scale_a))

    kb_c = to_heads(bk_c, B_HEADS)
    vb_c = to_heads(bv_c, B_HEADS)
    ob = from_heads(neighbourhood_attention(to_heads(bq, B_HEADS), to_heads(bk, B_HEADS), to_heads(bv, B_HEADS),
                                            kb_c, vb_c, na_rpb, rows))

    rep = C_HEADS // C_KV_HEADS
    scale_c = C_HEAD_DIM ** -0.5
    qc = apply_rope(rmsnorm(to_heads(cq, C_HEADS), g_qnorm), *rope_c)
    kc = apply_rope(rmsnorm(to_heads(ck, C_KV_HEADS), g_knorm), *rope_c)
    vc = to_heads(cv, C_KV_HEADS)
    kc_c = rmsnorm(to_heads(ck_c, C_KV_HEADS), g_knorm)
    vc_c = to_heads(cv_c, C_KV_HEADS)
    kc_all = jnp.repeat(jnp.concatenate([kc_c, kc], axis=2), rep, axis=1)
    vc_all = jnp.repeat(jnp.concatenate([vc_c, vc], axis=2), rep, axis=1)
    oc = from_heads(block_attention(qc[:, :, None], kc_all[:, :, None], vc_all, ones1, scale_c))

    scale_d = (MLA_NOPE + MLA_ROPE) ** -0.5

    def mla_q(qa_in, rope):
        q = to_heads(jnp.einsum('bsr,re->bse', rmsnorm(qa_in, g_q_lora), w_uq), D_HEADS)
        q_nope, q_rope = q[..., :MLA_NOPE], q[..., MLA_NOPE:]
        if rope is not None:
            q_rope = apply_rope(q_rope, *rope)
        return jnp.concatenate([q_nope, q_rope], axis=-1)

    def mla_kv(kva_in, kr_in, rope):
        kv = to_heads(jnp.einsum('bsr,re->bse', rmsnorm(kva_in, g_kv_lora), w_ukv), D_HEADS)
        k_nope, v = kv[..., :MLA_NOPE], kv[..., MLA_NOPE:]
        k_rope = kr_in[:, None]
        if rope is not None:
            k_rope = apply_rope(k_rope, *rope)
        k_rope = jnp.broadcast_to(k_rope, k_nope.shape[:-1] + (MLA_ROPE,))
        return jnp.concatenate([k_nope, k_rope], axis=-1), v

    qd = mla_q(dqa, rope_d)
    kd, vd = mla_kv(dkva, dkr, rope_d)
    kd_c, vd_c = mla_kv(dkva_c, dkr_c, None)
    od = from_heads(block_attention(qd[:, :, None], jnp.concatenate([kd_c, kd], axis=2)[:, :, None],
                                    jnp.concatenate([vd_c, vd], axis=2), ones1, scale_d))

    def merge(o_a, o_b, o_c, o_d, g):
        zb, zs, _ = g.shape
        gs = jax.nn.sigmoid(g.astype(f32)).astype(o_a.dtype).reshape(zb, zs, N_BRANCH, D_MODEL)
        m = (gs[:, :, 0] * jnp.einsum('bse,ed->bsd', o_a, w_br_a)
             + gs[:, :, 1] * jnp.einsum('bse,ed->bsd', o_b, w_br_b)
             + gs[:, :, 2] * jnp.einsum('bse,ed->bsd', o_c, w_br_c)
             + gs[:, :, 3] * jnp.einsum('bse,ed->bsd', o_d, w_br_d))
        return jnp.einsum('bsd,de->bse', m, w_out)

    y = merge(oa, ob, oc, od, gates)
    if not want_ctx:
        return y, None
    oa_c = diff_out(block_attention(diff_qk(aq_c), ka_c, va_c, coefs, scale_a))
    ob_c = from_heads(block_attention(to_heads(bq_c, B_HEADS)[:, :, None], kb_c[:, :, None], vb_c, ones1,
                                      B_HEAD_DIM ** -0.5))
    qc_c = rmsnorm(to_heads(cq_c, C_HEADS), g_qnorm)
    oc_c = from_heads(block_attention(qc_c[:, :, None], jnp.repeat(kc_c, rep, axis=1)[:, :, None],
                                      jnp.repeat(vc_c, rep, axis=1), ones1, scale_c))
    od_c = from_heads(block_attention(mla_q(dqa_c, None)[:, :, None], kd_c[:, :, None], vd_c, ones1, scale_d))
    y_c = merge(oa_c, ob_c, oc_c, od_c, gates_c)
    return y, y_c


def swiglu(h, w1, w3, w2):
    a = jnp.einsum('bsd,df->bsf', h, w1)
    b = jnp.einsum('bsd,df->bsf', h, w3)
    return jnp.einsum('bsf,fd->bsd', jax.nn.silu(a) * b, w2)


def moe_swiglu(h, w_router, w1, w3, w2):
    logits = jnp.einsum('bsd,de->bse', h, w_router, preferred_element_type=jnp.float32)
    top_v, top_i = lax.top_k(logits, TOP_K)
    wts = jax.nn.softmax(top_v, axis=-1)
    combine = jnp.sum(jax.nn.one_hot(top_i, N_EXPERTS, dtype=jnp.float32) * wts[..., None], axis=-2)
    combine = combine.astype(h.dtype)
    out = jnp.zeros_like(h)
    for e in range(N_EXPERTS):
        out = out + combine[..., e:e + 1] * swiglu(h, w1[e], w3[e], w2[e])
    return out


def setup_inputs(seed: int = 0) -> dict:
    key = jax.random.key(seed)
    ks = iter(jax.random.split(key, 48))

    def nrm(shape, scale):
        return jax.random.normal(next(ks), shape, jnp.float32) * scale

    def gain(shape):
        return 1.0 + nrm(shape, 0.05)

    L = DEPTH
    n_dense = (DEPTH + 1) // 2
    n_moe = DEPTH // 2
    D = D_MODEL
    return {
        'x': nrm((BATCH, SEQ, D), 1.0),
        'c': nrm((BATCH, D), 1.0),
        'ctx': nrm((BATCH, CTX_LEN, D), 1.0),
        'c_ctx': nrm((D,), 1.0),
        'w_ada': nrm((L, D, 6 * D), 0.5 * D ** -0.5),
        'b_ada': nrm((L, 6 * D), 0.01),
        'g_mix_pre': gain((L, D)),
        'g_mix_post': gain((L, D)),
        'g_ffn_pre': gain((L, D)),
        'g_ffn_post': gain((L, D)),
        'w_in': nrm((L, D, D_IN), D ** -0.5),
        'lam_q1': nrm((L, A_QK_DIM), 0.1),
        'lam_k1': nrm((L, A_QK_DIM), 0.1),
        'lam_q2': nrm((L, A_QK_DIM), 0.1),
        'lam_k2': nrm((L, A_QK_DIM), 0.1),
        'g_diff_sub': gain((L, A_V_DIM)),
        'na_rpb': nrm((L, B_HEADS, 2 * NA_KH - 1, 2 * NA_KW - 1), 0.1),
        'g_qnorm': gain((L, C_HEAD_DIM)),
        'g_knorm': gain((L, C_HEAD_DIM)),
        'g_q_lora': gain((L, MLA_Q_LORA)),
        'w_uq': nrm((L, MLA_Q_LORA, D_HEADS * (MLA_NOPE + MLA_ROPE)), MLA_Q_LORA ** -0.5),
        'g_kv_lora': gain((L, MLA_KV_LORA)),
        'w_ukv': nrm((L, MLA_KV_LORA, D_HEADS * (MLA_NOPE + MLA_V)), MLA_KV_LORA ** -0.5),
        'w_br_a': nrm((L, A_HEADS * A_V_DIM, D), (A_HEADS * A_V_DIM) ** -0.5),
        'w_br_b': nrm((L, B_HEADS * B_HEAD_DIM, D), (B_HEADS * B_HEAD_DIM) ** -0.5),
        'w_br_c': nrm((L, C_HEADS * C_HEAD_DIM, D), (C_HEADS * C_HEAD_DIM) ** -0.5),
        'w_br_d': nrm((L, D_HEADS * MLA_V, D), (D_HEADS * MLA_V) ** -0.5),
        'w_out': nrm((L, D, D), D ** -0.5),
        'w1_dense': nrm((n_dense, D, FFN_DENSE), D ** -0.5),
        'w3_dense': nrm((n_dense, D, FFN_DENSE), D ** -0.5),
        'w2_dense': nrm((n_dense, FFN_DENSE, D), FFN_DENSE ** -0.5),
        'w_router': nrm((n_moe, D, N_EXPERTS), D ** -0.5),
        'w1_moe': nrm((n_moe, N_EXPERTS, D, FFN_EXPERT), D ** -0.5),
        'w3_moe': nrm((n_moe, N_EXPERTS, D, FFN_EXPERT), D ** -0.5),
        'w2_moe': nrm((n_moe, N_EXPERTS, FFN_EXPERT, D), FFN_EXPERT ** -0.5),
    }


def reference(x, c, ctx, c_ctx, w_ada, b_ada, g_mix_pre, g_mix_post, g_ffn_pre, g_ffn_post,
              w_in, lam_q1, lam_k1, lam_q2, lam_k2, g_diff_sub, na_rpb, g_qnorm, g_knorm,
              g_q_lora, w_uq, g_kv_lora, w_ukv, w_br_a, w_br_b, w_br_c, w_br_d, w_out,
              w1_dense, w3_dense, w2_dense, w_router, w1_moe, w3_moe, w2_moe):
    S = x.shape[1]
    rows = S // GRID_W
    rope_a = axial_rope_tables(S, A_QK_DIM)
    rope_c = axial_rope_tables(S, C_HEAD_DIM)
    rope_d = axial_rope_tables(S, MLA_ROPE)
    s_c = jax.nn.silu(c)
    s_cc = jax.nn.silu(c_ctx)

    def channel_mixer(z, i):
        j = i // 2
        if i % 2 == 0:
            return swiglu(z, w1_dense[j], w3_dense[j], w2_dense[j])
        return moe_swiglu(z, w_router[j], w1_moe[j], w3_moe[j], w2_moe[j])

    for i in range(DEPTH):
        last = i == DEPTH - 1
        mod = jnp.einsum('bd,de->be', s_c, w_ada[i]) + b_ada[i]
        sh1, sc1, g1, sh2, sc2, g2 = [m[:, None, :] for m in jnp.split(mod, 6, axis=-1)]
        mod_c = jnp.einsum('d,de->e', s_cc, w_ada[i]) + b_ada[i]
        csh1, csc1, cg1, csh2, csc2, cg2 = jnp.split(mod_c, 6, axis=-1)
        lam_init = 0.8 - 0.6 * math.exp(-0.3 * i)

        h = rmsnorm(x, g_mix_pre[i]) * (1.0 + sc1) + sh1
        hc = rmsnorm(ctx, g_mix_pre[i]) * (1.0 + csc1) + csh1
        y, y_c = hybrid_mixer(h, hc, not last, rows, rope_a, rope_c, rope_d, lam_init,
                              w_in[i], lam_q1[i], lam_k1[i], lam_q2[i], lam_k2[i], g_diff_sub[i], na_rpb[i],
                              g_qnorm[i], g_knorm[i], g_q_lora[i], w_uq[i], g_kv_lora[i], w_ukv[i],
                              w_br_a[i], w_br_b[i], w_br_c[i], w_br_d[i], w_out[i])
        x = x + g1 * rmsnorm(y, g_mix_post[i])
        h = rmsnorm(x, g_ffn_pre[i]) * (1.0 + sc2) + sh2
        x = x + g2 * rmsnorm(channel_mixer(h, i), g_ffn_post[i])
        if not last:
            ctx = ctx + cg1 * rmsnorm(y_c, g_mix_post[i])
            hc = rmsnorm(ctx, g_ffn_pre[i]) * (1.0 + csc2) + csh2
            ctx = ctx + cg2 * rmsnorm(channel_mixer(hc, i), g_ffn_post[i])
    return x
```

```python
import functools
import math

import jax
import jax.numpy as jnp
from jax import lax
from jax.experimental import pallas as pl
from jax.experimental.pallas import tpu as pltpu

F32 = jnp.float32
BF16 = jnp.bfloat16
HIGHEST = lax.Precision.HIGHEST

D_MODEL = 1024
GRID_W = 64
ROPE_BASE = 10000.0
EPS = 1e-6
N_BRANCH = 4
A_HEADS = 4
A_QK_DIM = 64
A_V_DIM = 128
B_HEADS = 8
B_HEAD_DIM = 64
NA_KH = 8
NA_KW = 16
C_HEADS = 8
C_KV_HEADS = 2
C_HEAD_DIM = 64
D_HEADS = 8
MLA_Q_LORA = 256
MLA_KV_LORA = 128
MLA_NOPE = 64
MLA_ROPE = 32
MLA_V = 64
N_EXPERTS = 8

LANES = 128
HALF = LANES // 2
MASKED = -1e30
N_MOD = 6
MOD_ROWS = 16
VMEM_LIMIT = 56 * 1024 * 1024

NT_DIMS = (((1,), (1,)), ((), ()))


def _pick_tile(candidates, *dims):
    for t in candidates:
        if all(d % t == 0 for d in dims):
            return t
    raise ValueError(f"no tile in {candidates} divides {dims}")


def _params(sem):
    return pltpu.CompilerParams(dimension_semantics=sem, vmem_limit_bytes=VMEM_LIMIT)


def _rms(x, g):
    return x * lax.rsqrt(jnp.mean(x * x, axis=-1, keepdims=True) + EPS) * g


def _modulated_norm(x, g_pre, sc, sh):
    return _rms(x, g_pre) * (1.0 + sc) + sh


def _rope(x, cos, sin_a, sin_b, half):
    up = pltpu.roll(x, LANES - half, axis=1)
    dn = pltpu.roll(x, half, axis=1)
    return x * cos + up * sin_a + dn * sin_b


def _group_mean_sq(x):
    r = lax.broadcasted_iota(jnp.int32, (LANES, LANES), 0) // HALF
    c = lax.broadcasted_iota(jnp.int32, (LANES, LANES), 1) // HALF
    ones = jnp.where(r == c, 1.0, 0.0).astype(F32)
    ss = lax.dot_general(x * x, ones, (((1,), (0,)), ((), ())), precision=HIGHEST,
                         preferred_element_type=F32)
    return ss * (1.0 / HALF)


def _rope_tables(seq_len, d_rot, lane_off, pad_rows):
    t = jnp.arange(seq_len)
    row = (t // GRID_W).astype(F32)
    col = (t % GRID_W).astype(F32)
    d_ax = d_rot // 2
    half = d_ax // 2
    inv = ROPE_BASE ** (-jnp.arange(0, d_ax, 2, dtype=F32) / d_ax)
    ang_r = row[:, None] * inv[None]
    ang_c = col[:, None] * inv[None]
    ang = jnp.concatenate([ang_r, ang_r, ang_c, ang_c], axis=-1)
    cos, sin = jnp.cos(ang), jnp.sin(ang)
    first = (jnp.arange(d_rot) % d_ax) < half
    sin_a = jnp.where(first[None], -sin, 0.0)
    sin_b = jnp.where(first[None], 0.0, sin)
    reps = (LANES - lane_off) // d_rot if lane_off == 0 else 1

    def place(tab, fill):
        tab = jnp.tile(tab, (1, reps))
        full = jnp.full((seq_len, LANES), fill, F32)
        full = full.at[:, lane_off:lane_off + tab.shape[1]].set(tab)
        return jnp.concatenate([full, jnp.full((pad_rows, LANES), fill, F32)], axis=0)

    return place(cos, 1.0), place(sin_a, 0.0), place(sin_b, 0.0), half


def _mods_kernel(c_ref, w_ref, b_ref, o_ref):
    c = c_ref[...]
    s = c * jax.nn.sigmoid(c)
    o_ref[0] = lax.dot_general(s, w_ref[0], (((1,), (0,)), ((), ())), precision=HIGHEST,
                               preferred_element_type=F32) + b_ref[0]


def _mods(cvec, w_ada, b_ada):
    n_layers, d, n6 = w_ada.shape
    tn = n6 // 4
    out = pl.pallas_call(
        _mods_kernel,
        grid=(n_layers, n6 // tn),
        in_specs=[pl.BlockSpec((MOD_ROWS, d), lambda l, j: (0, 0)),
                  pl.BlockSpec((1, d, tn), lambda l, j: (l, 0, j)),
                  pl.BlockSpec((1, 1, tn), lambda l, j: (l, 0, j))],
        out_specs=pl.BlockSpec((1, MOD_ROWS, tn), lambda l, j: (l, 0, j)),
        out_shape=jax.ShapeDtypeStruct((n_layers, MOD_ROWS, n6), F32),
        compiler_params=_params(("arbitrary", "arbitrary")),
        name="adaln_mods",
    )(cvec, w_ada, b_ada.reshape(n_layers, 1, n6))
    return out.reshape(n_layers * MOD_ROWS * N_MOD, 1, d)


class _Rows:
    def __init__(self, batch, seq, ctx_len):
        self.B, self.S, self.C = batch, seq, ctx_len
        self.n_lat = batch * seq
        self.n_all = batch * (seq + ctx_len)

    def mod_spec(self, layer, k, tm):
        n_lat_tiles = self.n_lat // tm
        per = self.S // tm
        B = self.B

        def imap(i, *_):
            brow = jnp.where(i < n_lat_tiles, i // per, B)
            return ((layer * MOD_ROWS + brow) * N_MOD + k, 0, 0)

        return pl.BlockSpec((1, 1, D_MODEL), imap)


def _prenorm_kernel(x_ref, g_ref, sc_ref, sh_ref, h_ref):
    h_ref[...] = _modulated_norm(x_ref[...], g_ref[...], sc_ref[0], sh_ref[0]).astype(BF16)


def _prenorm(rows, x, g_pre, mods, layer, tm):
    n = x.shape[0]
    row = lambda i: (i, 0)
    return pl.pallas_call(
        _prenorm_kernel,
        grid=(n // tm,),
        in_specs=[pl.BlockSpec((tm, D_MODEL), row),
                  pl.BlockSpec((1, D_MODEL), lambda i: (0, 0)),
                  rows.mod_spec(layer, 1, tm), rows.mod_spec(layer, 0, tm)],
        out_specs=pl.BlockSpec((tm, D_MODEL), row),
        out_shape=jax.ShapeDtypeStruct((n, D_MODEL), BF16),
        compiler_params=_params(("parallel",)),
        name="prenorm",
    )(x, g_pre.reshape(1, -1), mods, mods)


def _proj_kernel(h_ref, w_ref, *rest, epi, half):
    acc = jnp.dot(h_ref[...], w_ref[...], preferred_element_type=F32)
    o_ref = rest[-1]
    if epi == "plain":
        o_ref[...] = acc.astype(BF16)
    elif epi == "sigmoid":
        o_ref[...] = jax.nn.sigmoid(acc).astype(BF16)
    elif epi == "rope":
        cos_ref, sa_ref, sb_ref = rest[:3]
        for s in range(acc.shape[1] // LANES):
            sl = slice(s * LANES, (s + 1) * LANES)
            o_ref[:, sl] = _rope(acc[:, sl], cos_ref[...], sa_ref[...], sb_ref[...], half).astype(BF16)
    elif epi == "headnorm_rope":
        cos_ref, sa_ref, sb_ref, g_ref = rest[:4]
        n_norm = g_ref.shape[1] // LANES
        for s in range(acc.shape[1] // LANES):
            sl = slice(s * LANES, (s + 1) * LANES)
            blk = acc[:, sl]
            if s < n_norm:
                blk = blk * lax.rsqrt(_group_mean_sq(blk) + EPS) * g_ref[:, sl]
                blk = _rope(blk, cos_ref[...], sa_ref[...], sb_ref[...], half)
            o_ref[:, sl] = blk.astype(BF16)


def _rope_spec(rows, tm):
    n_lat_tiles = rows.n_lat // tm
    per = rows.S // tm
    return pl.BlockSpec((tm, LANES), lambda i, j: (jnp.where(i < n_lat_tiles, i % per, per), 0))


def _proj(rows, h, w, tm, tn, epi, tables=None, gain=None, name="proj"):
    n, k = h.shape
    nout = w.shape[1]
    in_specs = [pl.BlockSpec((tm, k), lambda i, j: (i, 0)),
                pl.BlockSpec((k, tn), lambda i, j: (0, j))]
    args = [h, w]
    half = 0
    if tables is not None:
        cos, sa, sb, half = tables
        in_specs += [_rope_spec(rows, tm)] * 3
        args += [cos, sa, sb]
    if gain is not None:
        in_specs.append(pl.BlockSpec((1, gain.shape[1]), lambda i, j: (0, 0)))
        args.append(gain)
    return pl.pallas_call(
        functools.partial(_proj_kernel, epi=epi, half=half),
        grid=(n // tm, nout // tn),
        in_specs=in_specs,
        out_specs=pl.BlockSpec((tm, tn), lambda i, j: (i, j)),
        out_shape=jax.ShapeDtypeStruct((n, nout), BF16),
        compiler_params=_params(("parallel", "arbitrary")),
        name=name,
    )(*args)


def _mla_kernel(h_ref, w_ref, cos_ref, sa_ref, sb_ref, gq_ref, gkv_ref, wuq_ref, wuk_ref, wuv_ref,
                qd_ref, kd_ref, vd_ref, *, half, scale):
    acc = jnp.dot(h_ref[...], w_ref[...], preferred_element_type=F32)
    cos, sa, sb = cos_ref[...], sa_ref[...], sb_ref[...]
    qn = _rms(acc[:, :MLA_Q_LORA], gq_ref[...]).astype(BF16)
    kvn = _rms(acc[:, MLA_Q_LORA:MLA_Q_LORA + MLA_KV_LORA], gkv_ref[...]).astype(BF16)
    k_rope = _rope(acc[:, MLA_Q_LORA + MLA_KV_LORA:], cos, sa, sb, half)
    q = jnp.dot(qn, wuq_ref[...], preferred_element_type=F32)
    k = jnp.dot(kvn, wuk_ref[...], preferred_element_type=F32)
    for s in range(D_HEADS):
        sl = slice(s * LANES, (s + 1) * LANES)
        qd_ref[:, sl] = (_rope(q[:, sl], cos, sa, sb, half) * scale).astype(BF16)
        kd_ref[:, sl] = (k[:, sl] + k_rope).astype(BF16)
    vd_ref[...] = jnp.dot(kvn, wuv_ref[...], preferred_element_type=F32).astype(BF16)


def _mla_proj(rows, h, w5, tables, g_q, g_kv, wuq, wuk, wuv, tm):
    n, k = h.shape
    cos, sa, sb, half = tables
    const = lambda i, j: (0, 0)
    nq, nv = D_HEADS * LANES, D_HEADS * MLA_V
    return pl.pallas_call(
        functools.partial(_mla_kernel, half=half, scale=(MLA_NOPE + MLA_ROPE) ** -0.5),
        grid=(n // tm, 1),
        in_specs=[pl.BlockSpec((tm, k), lambda i, j: (i, 0)),
                  pl.BlockSpec(w5.shape, const),
                  _rope_spec(rows, tm), _rope_spec(rows, tm), _rope_spec(rows, tm),
                  pl.BlockSpec((1, MLA_Q_LORA), const), pl.BlockSpec((1, MLA_KV_LORA), const),
                  pl.BlockSpec(wuq.shape, const), pl.BlockSpec(wuk.shape, const), pl.BlockSpec(wuv.shape, const)],
        out_specs=[pl.BlockSpec((tm, nq), lambda i, j: (i, 0)),
                   pl.BlockSpec((tm, nq), lambda i, j: (i, 0)),
                   pl.BlockSpec((tm, nv), lambda i, j: (i, 0))],
        out_shape=[jax.ShapeDtypeStruct((n, nq), BF16), jax.ShapeDtypeStruct((n, nq), BF16),
                   jax.ShapeDtypeStruct((n, nv), BF16)],
        compiler_params=_params(("parallel", "arbitrary")),
        name="mla_proj",
    )(h, w5, cos, sa, sb, g_q.reshape(1, -1), g_kv.reshape(1, -1), wuq, wuk, wuv)


def _lo_mask(n):
    return lax.broadcasted_iota(jnp.int32, (n, LANES), 1) < HALF


def _softmax_pv(q, keys, vals, biases=None):
    scores = []
    for idx, k in enumerate(keys):
        s = lax.dot_general(q, k, NT_DIMS, preferred_element_type=F32)
        if biases is not None and biases[idx] is not None:
            s = s + biases[idx]
        scores.append(s)
    m = scores[0].max(axis=-1, keepdims=True)
    for s in scores[1:]:
        m = jnp.maximum(m, s.max(axis=-1, keepdims=True))
    denom = None
    out = None
    for s, v in zip(scores, vals):
        e = jnp.exp(s - m)
        d = e.sum(axis=-1, keepdims=True)
        o = jnp.dot(e.astype(BF16), v, preferred_element_type=F32)
        denom = d if denom is None else denom + d
        out = o if out is None else out + o
    return out / denom


def _pair_attention(q0, q1, keys0, keys1, vals, lo):
    o0 = _softmax_pv(q0, keys0, vals)
    o1 = _softmax_pv(q1, keys1, vals)
    return jnp.where(lo, o0, o1)


def _attn_kernel(*refs, mode, split, nq, want_ctx, lam_init):
    refs = list(refs)
    o_ref = refs.pop()
    if split:
        q_ref, kc_ref, kl_ref, vc_ref, vl_ref = refs[:5]
        extra = refs[5:]
        q = q_ref[...]
        lo = _lo_mask(q.shape[0])
        zero = jnp.zeros_like(q)
        q0, q1 = jnp.where(lo, q, zero), jnp.where(lo, zero, q)
        kc0 = kc1 = kc_ref
        kl0 = kl1 = kl_ref
    else:
        q0_ref, q1_ref, kc0, kc1, kl0, kl1, vc_ref, vl_ref = refs[:8]
        extra = refs[8:]
        q0, q1 = q0_ref[...], q1_ref[...]
        lo = _lo_mask(q0.shape[0])

    def finish(o0, o1):
        if mode == "diff":
            lq1, lk1, lq2, lk2, g_ref = extra
            lam = (jnp.exp(jnp.sum(lq1[...] * lk1[...], axis=-1, keepdims=True))
                   - jnp.exp(jnp.sum(lq2[...] * lk2[...], axis=-1, keepdims=True)) + lam_init)
            o = o0 - lam * o1
            o_ref[...] = (_rms(o, g_ref[...]) * (1.0 - lam_init)).astype(BF16)
        else:
            o_ref[...] = jnp.where(lo, o0, o1).astype(BF16)

    def latent_queries():
        vals = [vc_ref[...], vl_ref[...]]
        finish(_softmax_pv(q0, [kc0[...], kl0[...]], vals), _softmax_pv(q1, [kc1[...], kl1[...]], vals))

    def context_queries():
        vals = [vc_ref[...]]
        finish(_softmax_pv(q0, [kc0[...]], vals), _softmax_pv(q1, [kc1[...]], vals))

    if want_ctx:
        t = pl.program_id(2)
        pl.when(t < nq)(latent_queries)
        pl.when(t >= nq)(context_queries)
    else:
        latent_queries()


def _q_row_map(rows, tq, want_ctx):
    nq = rows.S // tq
    nc = rows.C // tq
    B = rows.B

    def qrow(b, t):
        if not want_ctx:
            return b * nq + t
        return jnp.where(t < nq, b * nq + t, B * nq + b * nc + (t - nq))

    return qrow, nq, nc


def _attention(rows, tq, want_ctx, mode, q_src, k_src, v_src, n_groups, extra=(), lam_init=0.0, name="attn"):
    B, S, C = rows.B, rows.S, rows.C
    qrow, nq, nc = _q_row_map(rows, tq, want_ctx)
    n_q_tiles = nq + (nc if want_ctx else 0)
    ctx_blk0 = rows.n_lat // C
    split = len(q_src) == 1
    in_specs, args = [], []
    for arr, colf in q_src:
        in_specs.append(pl.BlockSpec((tq, LANES), lambda b, g, t, colf=colf: (qrow(b, t), colf(g))))
        args.append(arr)
    for arr, colf in k_src:
        in_specs.append(pl.BlockSpec((C, LANES), lambda b, g, t, colf=colf: (ctx_blk0 + b, colf(g))))
        args.append(arr)
    for arr, colf in k_src:
        in_specs.append(pl.BlockSpec((S, LANES), lambda b, g, t, colf=colf: (b, colf(g))))
        args.append(arr)
    varr, vcolf = v_src
    in_specs.append(pl.BlockSpec((C, LANES), lambda b, g, t: (ctx_blk0 + b, vcolf(g))))
    in_specs.append(pl.BlockSpec((S, LANES), lambda b, g, t: (b, vcolf(g))))
    args += [varr, varr]
    for e in extra:
        in_specs.append(pl.BlockSpec(e.shape, lambda b, g, t: (0, 0)))
        args.append(e)
    n_out_rows = rows.n_all if want_ctx else rows.n_lat
    return pl.pallas_call(
        functools.partial(_attn_kernel, mode=mode, split=split, nq=nq, want_ctx=want_ctx, lam_init=lam_init),
        grid=(B, n_groups, n_q_tiles),
        in_specs=in_specs,
        out_specs=pl.BlockSpec((tq, LANES), lambda b, g, t: (qrow(b, t), g)),
        out_shape=jax.ShapeDtypeStruct((n_out_rows, n_groups * LANES), BF16),
        compiler_params=_params(("parallel", "parallel", "arbitrary")),
        name=name,
    )(*args)


def _nbr_kernel(q_ref, kc_ref, kl_ref, vc_ref, vl_ref, bias_ref, o_ref, *, n_row_blocks, rows_per_step,
                grid_rows, want_ctx):
    lo64 = _lo_mask(GRID_W)
    win = NA_KH * GRID_W

    def latent_queries():
        tb = pl.program_id(2)
        kc, vc = kc_ref[...], vc_ref[...]
        for g in range(rows_per_step):
            r = tb * rows_per_step + g
            start = jnp.clip(r - NA_KH // 2, 0, grid_rows - NA_KH)
            di = start - r + (NA_KH - 1)
            off = pl.multiple_of(start * GRID_W, GRID_W)
            kw = kl_ref[pl.ds(off, win), :]
            vw = vl_ref[pl.ds(off, win), :]
            q = q_ref[g * GRID_W:(g + 1) * GRID_W, :]
            zero = jnp.zeros_like(q)
            o0 = _softmax_pv(jnp.where(lo64, q, zero), [kc, kw], [vc, vw], [None, bias_ref[0, di]])
            o1 = _softmax_pv(jnp.where(lo64, zero, q), [kc, kw], [vc, vw], [None, bias_ref[1, di]])
            o_ref[g * GRID_W:(g + 1) * GRID_W, :] = jnp.where(lo64, o0, o1).astype(BF16)

    def context_queries():
        q = q_ref[...]
        lo = _lo_mask(q.shape[0])
        zero = jnp.zeros_like(q)
        kc, vc = kc_ref[...], vc_ref[...]
        o_ref[...] = _pair_attention(jnp.where(lo, q, zero), jnp.where(lo, zero, q), [kc], [kc], [vc], lo).astype(BF16)

    if want_ctx:
        t = pl.program_id(2)
        pl.when(t < n_row_blocks)(latent_queries)
        pl.when(t >= n_row_blocks)(context_queries)
    else:
        latent_queries()


def _nbr_bias(rpb):
    kw = NA_KW
    col = jnp.arange(GRID_W)
    col_start = jnp.clip(col - kw // 2, 0, GRID_W - kw)
    in_win = (col[None, :] >= col_start[:, None]) & (col[None, :] < col_start[:, None] + kw)
    dc = jnp.clip(col[None, :] - col[:, None], -(kw - 1), kw - 1) + (NA_KW - 1)
    dr = jnp.arange(NA_KH)[:, None] + jnp.arange(NA_KH)[None, :]
    bias = rpb[:, dr[:, None, :, None], dc[None, :, None, :]]
    bias = jnp.where(in_win[None, None, :, None, :], bias.astype(F32), MASKED)
    return bias.reshape(rpb.shape[0], NA_KH, GRID_W, NA_KH * GRID_W)


def _nbr_attention(rows, want_ctx, misc, bias, q_blk0, k_blk0, v_blk0):
    B, S, C = rows.B, rows.S, rows.C
    tq = 256
    rows_per_step = tq // GRID_W
    grid_rows = S // GRID_W
    assert grid_rows >= NA_KH and C % tq == 0 and S % tq == 0
    qrow, nq, nc = _q_row_map(rows, tq, want_ctx)
    n_q_tiles = nq + (nc if want_ctx else 0)
    ctx_blk0 = rows.n_lat // C
    n_groups = B_HEADS // 2
    n_out_rows = rows.n_all if want_ctx else rows.n_lat
    return pl.pallas_call(
        functools.partial(_nbr_kernel, n_row_blocks=nq, rows_per_step=rows_per_step, grid_rows=grid_rows,
                          want_ctx=want_ctx),
        grid=(B, n_groups, n_q_tiles),
        in_specs=[pl.BlockSpec((tq, LANES), lambda b, g, t: (qrow(b, t), q_blk0 + g)),
                  pl.BlockSpec((C, LANES), lambda b, g, t: (ctx_blk0 + b, k_blk0 + g)),
                  pl.BlockSpec((S, LANES), lambda b, g, t: (b, k_blk0 + g)),
                  pl.BlockSpec((C, LANES), lambda b, g, t: (ctx_blk0 + b, v_blk0 + g)),
                  pl.BlockSpec((S, LANES), lambda b, g, t: (b, v_blk0 + g)),
                  pl.BlockSpec((2, NA_KH, GRID_W, NA_KH * GRID_W), lambda b, g, t: (g, 0, 0, 0))],
        out_specs=pl.BlockSpec((tq, LANES), lambda b, g, t: (qrow(b, t), g)),
        out_shape=jax.ShapeDtypeStruct((n_out_rows, n_groups * LANES), BF16),
        compiler_params=_params(("parallel", "parallel", "arbitrary")),
        name="attn_nbr",
    )(misc, misc, misc, misc, misc, bias)


def _residual_epilogue(x, y, gate, g_post, nxt):
    x_new = x + gate * _rms(y, g_post)
    h = None
    if nxt is not None:
        g_pre, sc, sh = nxt
        h = _modulated_norm(x_new, g_pre, sc, sh)
    return x_new, h


def _route(h, w_router):
    logits = lax.dot_general(h, w_router, (((1,), (0,)), ((), ())), precision=HIGHEST,
                             preferred_element_type=F32)
    n_e = logits.shape[1]
    idx = lax.broadcasted_iota(jnp.int32, logits.shape, 1)
    top1 = logits.max(axis=-1, keepdims=True)
    i1 = jnp.where(logits == top1, idx, n_e).min(axis=-1, keepdims=True)
    rest = jnp.where(idx == i1, -jnp.inf, logits)
    top2 = rest.max(axis=-1, keepdims=True)
    i2 = jnp.where(rest == top2, idx, n_e).min(axis=-1, keepdims=True)
    e2 = jnp.exp(top2 - top1)
    w1 = 1.0 / (1.0 + e2)
    w2 = e2 / (1.0 + e2)
    return jnp.where(idx == i1, w1, 0.0) + jnp.where(idx == i2, w2, 0.0)


def _merge_kernel(gates_ref, oa_ref, ob_ref, oc_ref, od_ref, wbr_ref, wout_ref, x_ref, gate_ref, gpost_ref,
                  gpre_ref, sc_ref, sh_ref, *rest, routed):
    m = None
    for j, o_ref in enumerate((oa_ref, ob_ref, oc_ref, od_ref)):
        br = jnp.dot(o_ref[...], wbr_ref[j], preferred_element_type=F32)
        term = gates_ref[:, j * D_MODEL:(j + 1) * D_MODEL].astype(F32) * br
        m = term if m is None else m + term
    y = jnp.dot(m.astype(BF16), wout_ref[...], preferred_element_type=F32)
    x_new, h = _residual_epilogue(x_ref[...], y, gate_ref[0], gpost_ref[...],
                                  (gpre_ref[...], sc_ref[0], sh_ref[0]))
    if routed:
        wr_ref, x_out, h_out, comb_out = rest
        comb_out[...] = _route(h, wr_ref[...])
    else:
        x_out, h_out = rest
    x_out[...] = x_new
    h_out[...] = h.astype(BF16)


def _merge(rows, n_rows, gates, o_all, wbr, wout, x, mods, layer, g_post, g_pre, w_router, tm):
    row = lambda i: (i, 0)
    const2 = lambda i: (0, 0)
    routed = w_router is not None
    in_specs = [pl.BlockSpec((tm, N_BRANCH * D_MODEL), row)]
    in_specs += [pl.BlockSpec((tm, o.shape[1]), row) for o in o_all]
    in_specs += [pl.BlockSpec(wbr.shape, lambda i: (0, 0, 0)), pl.BlockSpec(wout.shape, const2),
                 pl.BlockSpec((tm, D_MODEL), row), rows.mod_spec(layer, 2, tm),
                 pl.BlockSpec((1, D_MODEL), const2), pl.BlockSpec((1, D_MODEL), const2),
                 rows.mod_spec(layer, 4, tm), rows.mod_spec(layer, 3, tm)]
    args = [gates, *o_all, wbr, wout, x, mods, g_post.reshape(1, -1), g_pre.reshape(1, -1), mods, mods]
    out_specs = [pl.BlockSpec((tm, D_MODEL), row), pl.BlockSpec((tm, D_MODEL), row)]
    out_shape = [jax.ShapeDtypeStruct((n_rows, D_MODEL), F32), jax.ShapeDtypeStruct((n_rows, D_MODEL), BF16)]
    if routed:
        in_specs.append(pl.BlockSpec(w_router.shape, const2))
        args.append(w_router)
        out_specs.append(pl.BlockSpec((tm, N_EXPERTS), row))
        out_shape.append(jax.ShapeDtypeStruct((n_rows, N_EXPERTS), F32))
    return pl.pallas_call(
        functools.partial(_merge_kernel, routed=routed),
        grid=(n_rows // tm,),
        in_specs=in_specs, out_specs=out_specs, out_shape=out_shape,
        compiler_params=_params(("parallel",)),
        name="merge_out",
    )(*args)


def _ffn_kernel(h_ref, w1_ref, w3_ref, w2_ref, x_ref, gate_ref, gpost_ref, *rest, has_next):
    f = pl.program_id(1)
    if has_next:
        gpre_ref, sc_ref, sh_ref, x_out, h_out, acc_ref = rest
    else:
        x_out, acc_ref = rest
    h = h_ref[...]
    a = jnp.dot(h, w1_ref[...], preferred_element_type=F32)
    b = jnp.dot(h, w3_ref[...], preferred_element_type=F32)
    part = jnp.dot((a * jax.nn.sigmoid(a) * b).astype(BF16), w2_ref[...], preferred_element_type=F32)

    @pl.when(f == 0)
    def _():
        acc_ref[...] = part

    @pl.when(f > 0)
    def _():
        acc_ref[...] += part

    @pl.when(f == pl.num_programs(1) - 1)
    def _():
        nxt = (gpre_ref[...], sc_ref[0], sh_ref[0]) if has_next else None
        x_new, hn = _residual_epilogue(x_ref[...], acc_ref[...], gate_ref[0], gpost_ref[...], nxt)
        x_out[...] = x_new
        if has_next:
            h_out[...] = hn.astype(BF16)


def _ffn(rows, n_rows, h, w1, w3, w2, x, mods, layer, g_post, g_pre_next, tm, tf):
    row = lambda i, f: (i, 0)
    const2 = lambda i, f: (0, 0)
    has_next = g_pre_next is not None
    ff = w1.shape[1]
    in_specs = [pl.BlockSpec((tm, D_MODEL), row),
                pl.BlockSpec((D_MODEL, tf), lambda i, f: (0, f)), pl.BlockSpec((D_MODEL, tf), lambda i, f: (0, f)),
                pl.BlockSpec((tf, D_MODEL), lambda i, f: (f, 0)),
                pl.BlockSpec((tm, D_MODEL), row), rows.mod_spec(layer, 5, tm), pl.BlockSpec((1, D_MODEL), const2)]
    args = [h, w1, w3, w2, x, mods, g_post.reshape(1, -1)]
    out_specs = [pl.BlockSpec((tm, D_MODEL), row)]
    out_shape = [jax.ShapeDtypeStruct((n_rows, D_MODEL), F32)]
    if has_next:
        in_specs += [pl.BlockSpec((1, D_MODEL), const2), rows.mod_spec(layer + 1, 1, tm),
                     rows.mod_spec(layer + 1, 0, tm)]
        args += [g_pre_next.reshape(1, -1), mods, mods]
        out_specs.append(pl.BlockSpec((tm, D_MODEL), row))
        out_shape.append(jax.ShapeDtypeStruct((n_rows, D_MODEL), BF16))
    return pl.pallas_call(
        functools.partial(_ffn_kernel, has_next=has_next),
        grid=(n_rows // tm, ff // tf),
        in_specs=in_specs, out_specs=out_specs, out_shape=out_shape,
        scratch_shapes=[pltpu.VMEM((tm, D_MODEL), F32)],
        compiler_params=_params(("parallel", "arbitrary")),
        name="ffn_dense",
    )(*args)


def _moe_kernel(h_ref, comb_ref, w1_ref, w3_ref, w2_ref, x_ref, gate_ref, gpost_ref, *rest, has_next):
    e = pl.program_id(1)
    f = pl.program_id(2)
    if has_next:
        gpre_ref, sc_ref, sh_ref, x_out, h_out, acc_ref = rest
    else:
        x_out, acc_ref = rest
    h = h_ref[...]
    a = jnp.dot(h, w1_ref[0], preferred_element_type=F32)
    b = jnp.dot(h, w3_ref[0], preferred_element_type=F32)
    comb = comb_ref[...]
    lane = lax.broadcasted_iota(jnp.int32, comb.shape, 1)
    cw = jnp.sum(jnp.where(lane == e, comb, 0.0), axis=-1, keepdims=True)
    part = jnp.dot((cw * (a * jax.nn.sigmoid(a) * b)).astype(BF16), w2_ref[0], preferred_element_type=F32)
    first = jnp.logical_and(e == 0, f == 0)

    @pl.when(first)
    def _():
        acc_ref[...] = part

    @pl.when(jnp.logical_not(first))
    def _():
        acc_ref[...] += part

    @pl.when(jnp.logical_and(e == pl.num_programs(1) - 1, f == pl.num_programs(2) - 1))
    def _():
        nxt = (gpre_ref[...], sc_ref[0], sh_ref[0]) if has_next else None
        x_new, hn = _residual_epilogue(x_ref[...], acc_ref[...], gate_ref[0], gpost_ref[...], nxt)
        x_out[...] = x_new
        if has_next:
            h_out[...] = hn.astype(BF16)


def _moe(rows, n_rows, h, comb, w1, w3, w2, x, mods, layer, g_post, g_pre_next, tm, tf):
    row = lambda i, e, f: (i, 0)
    const2 = lambda i, e, f: (0, 0)
    has_next = g_pre_next is not None
    n_e, _, ff = w1.shape
    in_specs = [pl.BlockSpec((tm, D_MODEL), row), pl.BlockSpec((tm, n_e), row),
                pl.BlockSpec((1, D_MODEL, tf), lambda i, e, f: (e, 0, f)),
                pl.BlockSpec((1, D_MODEL, tf), lambda i, e, f: (e, 0, f)),
                pl.BlockSpec((1, tf, D_MODEL), lambda i, e, f: (e, f, 0)),
                pl.BlockSpec((tm, D_MODEL), row), rows.mod_spec(layer, 5, tm), pl.BlockSpec((1, D_MODEL), const2)]
    args = [h, comb, w1, w3, w2, x, mods, g_post.reshape(1, -1)]
    out_specs = [pl.BlockSpec((tm, D_MODEL), row)]
    out_shape = [jax.ShapeDtypeStruct((n_rows, D_MODEL), F32)]
    if has_next:
        in_specs += [pl.BlockSpec((1, D_MODEL), const2), rows.mod_spec(layer + 1, 1, tm),
                     rows.mod_spec(layer + 1, 0, tm)]
        args += [g_pre_next.reshape(1, -1), mods, mods]
        out_specs.append(pl.BlockSpec((tm, D_MODEL), row))
        out_shape.append(jax.ShapeDtypeStruct((n_rows, D_MODEL), BF16))
    return pl.pallas_call(
        functools.partial(_moe_kernel, has_next=has_next),
        grid=(n_rows // tm, n_e, ff // tf),
        in_specs=in_specs, out_specs=out_specs, out_shape=out_shape,
        scratch_shapes=[pltpu.VMEM((tm, D_MODEL), F32)],
        compiler_params=_params(("parallel", "arbitrary", "arbitrary")),
        name="ffn_moe",
    )(*args)


def _in_proj_weights(w_in):
    widths = (512, 512, 512, 512, 512, 512, 512, 128, 128, MLA_Q_LORA, MLA_KV_LORA, MLA_ROPE, N_BRANCH * D_MODEL)
    cuts, o = [], 0
    for w in widths:
        cuts.append((o, o + w))
        o += w
    aq, ak, av, bq, bk, bv, cq, ck, cv, dqa, dkva, dkr, gates = [w_in[:, a:b] for a, b in cuts]
    cq = cq.reshape(-1, 2, 4, C_HEAD_DIM).transpose(0, 2, 1, 3).reshape(-1, C_HEADS * C_HEAD_DIM)
    dkr_pad = jnp.zeros((w_in.shape[0], LANES), w_in.dtype).at[:, MLA_NOPE:MLA_NOPE + MLA_ROPE].set(dkr)
    scale = A_QK_DIM ** -0.5
    w1 = jnp.concatenate([aq * scale, ak], axis=1)
    w2 = jnp.concatenate([av, bq * (B_HEAD_DIM ** -0.5), bk, bv], axis=1)
    w3 = jnp.concatenate([cq, ck, cv], axis=1)
    w5 = jnp.concatenate([dqa, dkva, dkr_pad], axis=1)
    return [w.astype(BF16) for w in (w1, w2, w3, gates, w5)]


def _mla_weights(w_uq, w_ukv):
    dq = MLA_NOPE + MLA_ROPE
    wuq = jnp.pad(w_uq.reshape(-1, D_HEADS, dq), ((0, 0), (0, 0), (0, LANES - dq))).reshape(-1, D_HEADS * LANES)
    kv = w_ukv.reshape(-1, D_HEADS, MLA_NOPE + MLA_V)
    wuk = jnp.pad(kv[:, :, :MLA_NOPE], ((0, 0), (0, 0), (0, LANES - MLA_NOPE))).reshape(-1, D_HEADS * LANES)
    wuv = kv[:, :, MLA_NOPE:].reshape(-1, D_HEADS * MLA_V)
    return wuq.astype(BF16), wuk.astype(BF16), wuv.astype(BF16)


def kernel(x, c, ctx, c_ctx, w_ada, b_ada, g_mix_pre, g_mix_post, g_ffn_pre, g_ffn_post, w_in, lam_q1, lam_k1, lam_q2, lam_k2, g_diff_sub, na_rpb, g_qnorm, g_knorm, g_q_lora, w_uq, g_kv_lora, w_ukv, w_br_a, w_br_b, w_br_c, w_br_d, w_out, w1_dense, w3_dense, w2_dense, w_router, w1_moe, w3_moe, w2_moe):
    B, S, D = x.shape
    C = ctx.shape[1]
    depth = w_in.shape[0]
    assert D == D_MODEL and B + 1 <= MOD_ROWS
    rows = _Rows(B, S, C)
    tq = 256
    tm_proj = _pick_tile((1024, 512, 256), S, B * C)
    tm_row = _pick_tile((512, 256), S, B * C)

    cvec = jnp.zeros((MOD_ROWS, D), F32).at[:B].set(c).at[B].set(c_ctx)
    mods = _mods(cvec, w_ada, b_ada)

    rope_qk = _rope_tables(S, A_QK_DIM, 0, tm_proj)
    rope_mla = _rope_tables(S, MLA_ROPE, MLA_NOPE, tm_proj)

    xs = jnp.concatenate([x.reshape(B * S, D), ctx.reshape(B * C, D)], axis=0)
    h = _prenorm(rows, xs, g_mix_pre[0], mods, 0, tm_proj)

    for i in range(depth):
        last = i == depth - 1
        want_ctx = not last
        n_rows = rows.n_all if want_ctx else rows.n_lat
        lam_init = 0.8 - 0.6 * math.exp(-0.3 * i)

        w1p, w2p, w3p, w4p, w5p = _in_proj_weights(w_in[i])
        qka = _proj(rows, h, w1p, tm_proj, 512, "rope", tables=rope_qk, name="proj_a_qk")
        misc = _proj(rows, h, w2p, tm_proj, 512, "plain", name="proj_misc")
        gain_c = jnp.concatenate([jnp.tile(g_qnorm[i], C_HEADS) * (C_HEAD_DIM ** -0.5),
                                  jnp.tile(g_knorm[i], C_KV_HEADS)]).reshape(1, -1).astype(F32)
        qkc = _proj(rows, h, w3p, tm_proj, w3p.shape[1], "headnorm_rope", tables=rope_qk, gain=gain_c,
                    name="proj_c")
        gates = _proj(rows, h, w4p, tm_proj, 512, "sigmoid", name="proj_gates")
        wuq, wuk, wuv = _mla_weights(w_uq[i], w_ukv[i])
        qd, kd, vd = _mla_proj(rows, h, w5p, rope_mla, g_q_lora[i], g_kv_lora[i], wuq, wuk, wuv, tm_row)

        lam_vecs = [v[i].reshape(1, -1).astype(F32) for v in (lam_q1, lam_k1, lam_q2, lam_k2)]
        oa = _attention(rows, tq, want_ctx, "diff",
                        q_src=[(qka, lambda g: g)], k_src=[(qka, lambda g: A_HEADS + g)],
                        v_src=(misc, lambda g: g), n_groups=A_HEADS,
                        extra=lam_vecs + [g_diff_sub[i].reshape(1, -1).astype(F32)], lam_init=lam_init,
                        name="attn_diff")
        ob = _nbr_attention(rows, want_ctx, misc, _nbr_bias(na_rpb[i]), q_blk0=4, k_blk0=8, v_blk0=12)
        oc = _attention(rows, tq, want_ctx, "select",
                        q_src=[(qkc, lambda g: g)], k_src=[(qkc, lambda g: 4)],
                        v_src=(qkc, lambda g: 5), n_groups=C_HEADS // 2, name="attn_gqa")
        od = _attention(rows, tq, want_ctx, "select",
                        q_src=[(qd, lambda g: 2 * g), (qd, lambda g: 2 * g + 1)],
                        k_src=[(kd, lambda g: 2 * g), (kd, lambda g: 2 * g + 1)],
                        v_src=(vd, lambda g: g), n_groups=D_HEADS // 2, name="attn_mla")

        wbc = w_br_c[i].reshape(2, 4, C_HEAD_DIM, D).transpose(1, 0, 2, 3).reshape(C_HEADS * C_HEAD_DIM, D)
        wbr = jnp.stack([w_br_a[i], w_br_b[i], wbc, w_br_d[i]]).astype(BF16)
        moe = i % 2 == 1
        j = i // 2
        merged = _merge(rows, n_rows, gates, (oa, ob, oc, od), wbr, w_out[i].astype(BF16), xs, mods, i,
                        g_mix_post[i], g_ffn_pre[i], w_router[j] if moe else None, tm_row)
        g_pre_next = None if last else g_mix_pre[i + 1]
        if moe:
            xs, h2, comb = merged
            outs = _moe(rows, n_rows, h2, comb, w1_moe[j].astype(BF16), w3_moe[j].astype(BF16),
                        w2_moe[j].astype(BF16), xs, mods, i, g_ffn_post[i], g_pre_next, tm_proj, 896)
        else:
            xs, h2 = merged
            outs = _ffn(rows, n_rows, h2, w1_dense[j].astype(BF16), w3_dense[j].astype(BF16),
                        w2_dense[j].astype(BF16), xs, mods, i, g_ffn_post[i], g_pre_next, tm_row, 1408)
        if last:
            xs = outs[0]
        else:
            xs, h = outs
    return xs[:B * S].reshape(B, S, D)
```

```python
import functools
import math

import jax
import jax.numpy as jnp
from jax import lax
from jax.experimental import pallas as pl
from jax.experimental.pallas import tpu as pltpu

F32 = jnp.float32
BF16 = jnp.bfloat16
HIGHEST = lax.Precision.HIGHEST

D_MODEL = 1024
GRID_W = 64
ROPE_BASE = 10000.0
EPS = 1e-6
N_BRANCH = 4
A_HEADS = 4
A_QK_DIM = 64
A_V_DIM = 128
B_HEADS = 8
B_HEAD_DIM = 64
NA_KH = 8
NA_KW = 16
C_HEADS = 8
C_KV_HEADS = 2
C_HEAD_DIM = 64
D_HEADS = 8
MLA_Q_LORA = 256
MLA_KV_LORA = 128
MLA_NOPE = 64
MLA_ROPE = 32
MLA_V = 64
N_EXPERTS = 8

LANES = 128
HALF = LANES // 2
MASKED = -1e30
N_MOD = 6
MOD_ROWS = 16
MOE_TILE = 1024
VMEM_LIMIT = 56 * 1024 * 1024

NT_DIMS = (((1,), (1,)), ((), ()))


def _pick_tile(candidates, *dims):
    for t in candidates:
        if all(d % t == 0 for d in dims):
            return t
    raise ValueError(f"no tile in {candidates} divides {dims}")


def _params(sem):
    return pltpu.CompilerParams(dimension_semantics=sem, vmem_limit_bytes=VMEM_LIMIT)


def _rms(x, g):
    return x * lax.rsqrt(jnp.mean(x * x, axis=-1, keepdims=True) + EPS) * g


def _modulated_norm(x, g_pre, sc, sh):
    return _rms(x, g_pre) * (1.0 + sc) + sh


def _rope(x, cos, sin_a, sin_b, half):
    up = pltpu.roll(x, LANES - half, axis=1)
    dn = pltpu.roll(x, half, axis=1)
    return x * cos + up * sin_a + dn * sin_b


def _group_mean_sq(x):
    r = lax.broadcasted_iota(jnp.int32, (LANES, LANES), 0) // HALF
    c = lax.broadcasted_iota(jnp.int32, (LANES, LANES), 1) // HALF
    ones = jnp.where(r == c, 1.0, 0.0).astype(F32)
    ss = lax.dot_general(x * x, ones, (((1,), (0,)), ((), ())), precision=HIGHEST,
                         preferred_element_type=F32)
    return ss * (1.0 / HALF)


def _rope_tables(seq_len, d_rot, lane_off, pad_rows):
    t = jnp.arange(seq_len)
    row = (t // GRID_W).astype(F32)
    col = (t % GRID_W).astype(F32)
    d_ax = d_rot // 2
    half = d_ax // 2
    inv = ROPE_BASE ** (-jnp.arange(0, d_ax, 2, dtype=F32) / d_ax)
    ang_r = row[:, None] * inv[None]
    ang_c = col[:, None] * inv[None]
    ang = jnp.concatenate([ang_r, ang_r, ang_c, ang_c], axis=-1)
    cos, sin = jnp.cos(ang), jnp.sin(ang)
    first = (jnp.arange(d_rot) % d_ax) < half
    sin_a = jnp.where(first[None], -sin, 0.0)
    sin_b = jnp.where(first[None], 0.0, sin)
    reps = (LANES - lane_off) // d_rot if lane_off == 0 else 1

    def place(tab, fill):
        tab = jnp.tile(tab, (1, reps))
        full = jnp.full((seq_len, LANES), fill, F32)
        full = full.at[:, lane_off:lane_off + tab.shape[1]].set(tab)
        return jnp.concatenate([full, jnp.full((pad_rows, LANES), fill, F32)], axis=0)

    return place(cos, 1.0), place(sin_a, 0.0), place(sin_b, 0.0), half


def _mods_kernel(c_ref, w_ref, b_ref, o_ref):
    c = c_ref[...]
    s = c * jax.nn.sigmoid(c)
    o_ref[0] = lax.dot_general(s, w_ref[0], (((1,), (0,)), ((), ())), precision=HIGHEST,
                               preferred_element_type=F32) + b_ref[0]


def _mods(cvec, w_ada, b_ada):
    n_layers, d, n6 = w_ada.shape
    tn = n6 // 4
    out = pl.pallas_call(
        _mods_kernel,
        grid=(n_layers, n6 // tn),
        in_specs=[pl.BlockSpec((MOD_ROWS, d), lambda l, j: (0, 0)),
                  pl.BlockSpec((1, d, tn), lambda l, j: (l, 0, j)),
                  pl.BlockSpec((1, 1, tn), lambda l, j: (l, 0, j))],
        out_specs=pl.BlockSpec((1, MOD_ROWS, tn), lambda l, j: (l, 0, j)),
        out_shape=jax.ShapeDtypeStruct((n_layers, MOD_ROWS, n6), F32),
        compiler_params=_params(("arbitrary", "arbitrary")),
        name="adaln_mods",
    )(cvec, w_ada, b_ada.reshape(n_layers, 1, n6))
    return out.reshape(n_layers * MOD_ROWS * N_MOD, 1, d)


class _Rows:
    def __init__(self, batch, seq, ctx_len):
        self.B, self.S, self.C = batch, seq, ctx_len
        self.n_lat = batch * seq
        self.n_all = batch * (seq + ctx_len)

    def mod_spec(self, layer, k, tm):
        n_lat_tiles = self.n_lat // tm
        per = self.S // tm
        B = self.B

        def imap(i, *_):
            brow = jnp.where(i < n_lat_tiles, i // per, B)
            return ((layer * MOD_ROWS + brow) * N_MOD + k, 0, 0)

        return pl.BlockSpec((1, 1, D_MODEL), imap)


def _prenorm_kernel(x_ref, g_ref, sc_ref, sh_ref, h_ref):
    h_ref[...] = _modulated_norm(x_ref[...], g_ref[...], sc_ref[0], sh_ref[0]).astype(BF16)


def _prenorm(rows, x, g_pre, mods, layer, tm):
    n = x.shape[0]
    row = lambda i: (i, 0)
    return pl.pallas_call(
        _prenorm_kernel,
        grid=(n // tm,),
        in_specs=[pl.BlockSpec((tm, D_MODEL), row),
                  pl.BlockSpec((1, D_MODEL), lambda i: (0, 0)),
                  rows.mod_spec(layer, 1, tm), rows.mod_spec(layer, 0, tm)],
        out_specs=pl.BlockSpec((tm, D_MODEL), row),
        out_shape=jax.ShapeDtypeStruct((n, D_MODEL), BF16),
        compiler_params=_params(("parallel",)),
        name="prenorm",
    )(x, g_pre.reshape(1, -1), mods, mods)


def _proj_kernel(h_ref, w_ref, *rest, epi, half):
    acc = jnp.dot(h_ref[...], w_ref[...], preferred_element_type=F32)
    o_ref = rest[-1]
    if epi == "plain":
        o_ref[...] = acc.astype(BF16)
    elif epi == "sigmoid":
        o_ref[...] = jax.nn.sigmoid(acc).astype(BF16)
    elif epi == "rope":
        cos_ref, sa_ref, sb_ref = rest[:3]
        for s in range(acc.shape[1] // LANES):
            sl = slice(s * LANES, (s + 1) * LANES)
            o_ref[:, sl] = _rope(acc[:, sl], cos_ref[...], sa_ref[...], sb_ref[...], half).astype(BF16)
    elif epi == "headnorm_rope":
        cos_ref, sa_ref, sb_ref, g_ref = rest[:4]
        n_norm = g_ref.shape[1] // LANES
        for s in range(acc.shape[1] // LANES):
            sl = slice(s * LANES, (s + 1) * LANES)
            blk = acc[:, sl]
            if s < n_norm:
                blk = blk * lax.rsqrt(_group_mean_sq(blk) + EPS) * g_ref[:, sl]
                blk = _rope(blk, cos_ref[...], sa_ref[...], sb_ref[...], half)
            o_ref[:, sl] = blk.astype(BF16)


def _rope_spec(rows, tm):
    n_lat_tiles = rows.n_lat // tm
    per = rows.S // tm
    return pl.BlockSpec((tm, LANES), lambda i, j: (jnp.where(i < n_lat_tiles, i % per, per), 0))


def _proj(rows, h, w, tm, tn, epi, tables=None, gain=None, name="proj"):
    n, k = h.shape
    nout = w.shape[1]
    in_specs = [pl.BlockSpec((tm, k), lambda i, j: (i, 0)),
                pl.BlockSpec((k, tn), lambda i, j: (0, j))]
    args = [h, w]
    half = 0
    if tables is not None:
        cos, sa, sb, half = tables
        in_specs += [_rope_spec(rows, tm)] * 3
        args += [cos, sa, sb]
    if gain is not None:
        in_specs.append(pl.BlockSpec((1, gain.shape[1]), lambda i, j: (0, 0)))
        args.append(gain)
    return pl.pallas_call(
        functools.partial(_proj_kernel, epi=epi, half=half),
        grid=(n // tm, nout // tn),
        in_specs=in_specs,
        out_specs=pl.BlockSpec((tm, tn), lambda i, j: (i, j)),
        out_shape=jax.ShapeDtypeStruct((n, nout), BF16),
        compiler_params=_params(("parallel", "arbitrary")),
        name=name,
    )(*args)


def _mla_kernel(h_ref, w_ref, cos_ref, sa_ref, sb_ref, gq_ref, gkv_ref, wuq_ref, wuk_ref, wuv_ref,
                qd_ref, kd_ref, vd_ref, *, half, scale):
    acc = jnp.dot(h_ref[...], w_ref[...], preferred_element_type=F32)
    cos, sa, sb = cos_ref[...], sa_ref[...], sb_ref[...]
    qn = _rms(acc[:, :MLA_Q_LORA], gq_ref[...]).astype(BF16)
    kvn = _rms(acc[:, MLA_Q_LORA:MLA_Q_LORA + MLA_KV_LORA], gkv_ref[...]).astype(BF16)
    k_rope = _rope(acc[:, MLA_Q_LORA + MLA_KV_LORA:], cos, sa, sb, half)
    q = jnp.dot(qn, wuq_ref[...], preferred_element_type=F32)
    k = jnp.dot(kvn, wuk_ref[...], preferred_element_type=F32)
    for s in range(D_HEADS):
        sl = slice(s * LANES, (s + 1) * LANES)
        qd_ref[:, sl] = (_rope(q[:, sl], cos, sa, sb, half) * scale).astype(BF16)
        kd_ref[:, sl] = (k[:, sl] + k_rope).astype(BF16)
    vd_ref[...] = jnp.dot(kvn, wuv_ref[...], preferred_element_type=F32).astype(BF16)


def _mla_proj(rows, h, w5, tables, g_q, g_kv, wuq, wuk, wuv, tm):
    n, k = h.shape
    cos, sa, sb, half = tables
    const = lambda i, j: (0, 0)
    nq, nv = D_HEADS * LANES, D_HEADS * MLA_V
    return pl.pallas_call(
        functools.partial(_mla_kernel, half=half, scale=(MLA_NOPE + MLA_ROPE) ** -0.5),
        grid=(n // tm, 1),
        in_specs=[pl.BlockSpec((tm, k), lambda i, j: (i, 0)),
                  pl.BlockSpec(w5.shape, const),
                  _rope_spec(rows, tm), _rope_spec(rows, tm), _rope_spec(rows, tm),
                  pl.BlockSpec((1, MLA_Q_LORA), const), pl.BlockSpec((1, MLA_KV_LORA), const),
                  pl.BlockSpec(wuq.shape, const), pl.BlockSpec(wuk.shape, const), pl.BlockSpec(wuv.shape, const)],
        out_specs=[pl.BlockSpec((tm, nq), lambda i, j: (i, 0)),
                   pl.BlockSpec((tm, nq), lambda i, j: (i, 0)),
                   pl.BlockSpec((tm, nv), lambda i, j: (i, 0))],
        out_shape=[jax.ShapeDtypeStruct((n, nq), BF16), jax.ShapeDtypeStruct((n, nq), BF16),
                   jax.ShapeDtypeStruct((n, nv), BF16)],
        compiler_params=_params(("parallel", "arbitrary")),
        name="mla_proj",
    )(h, w5, cos, sa, sb, g_q.reshape(1, -1), g_kv.reshape(1, -1), wuq, wuk, wuv)


def _lo_mask(n):
    return lax.broadcasted_iota(jnp.int32, (n, LANES), 1) < HALF


def _softmax_pv(q, keys, vals, biases=None):
    scores = []
    for idx, k in enumerate(keys):
        s = lax.dot_general(q, k, NT_DIMS, preferred_element_type=F32)
        if biases is not None and biases[idx] is not None:
            s = s + biases[idx]
        scores.append(s)
    m = scores[0].max(axis=-1, keepdims=True)
    for s in scores[1:]:
        m = jnp.maximum(m, s.max(axis=-1, keepdims=True))
    denom = None
    out = None
    for s, v in zip(scores, vals):
        e = jnp.exp(s - m)
        d = e.sum(axis=-1, keepdims=True)
        o = jnp.dot(e.astype(BF16), v, preferred_element_type=F32)
        denom = d if denom is None else denom + d
        out = o if out is None else out + o
    return out / denom


def _pair_attention(q0, q1, keys0, keys1, vals, lo):
    o0 = _softmax_pv(q0, keys0, vals)
    o1 = _softmax_pv(q1, keys1, vals)
    return jnp.where(lo, o0, o1)


def _attn_kernel(*refs, mode, split, nq, want_ctx, lam_init):
    refs = list(refs)
    o_ref = refs.pop()
    if split:
        q_ref, kc_ref, kl_ref, vc_ref, vl_ref = refs[:5]
        extra = refs[5:]
        q = q_ref[...]
        lo = _lo_mask(q.shape[0])
        zero = jnp.zeros_like(q)
        q0, q1 = jnp.where(lo, q, zero), jnp.where(lo, zero, q)
        kc0 = kc1 = kc_ref
        kl0 = kl1 = kl_ref
    else:
        q0_ref, q1_ref, kc0, kc1, kl0, kl1, vc_ref, vl_ref = refs[:8]
        extra = refs[8:]
        q0, q1 = q0_ref[...], q1_ref[...]
        lo = _lo_mask(q0.shape[0])

    def finish(o0, o1):
        if mode == "diff":
            lq1, lk1, lq2, lk2, g_ref = extra
            lam = (jnp.exp(jnp.sum(lq1[...] * lk1[...], axis=-1, keepdims=True))
                   - jnp.exp(jnp.sum(lq2[...] * lk2[...], axis=-1, keepdims=True)) + lam_init)
            o = o0 - lam * o1
            o_ref[...] = (_rms(o, g_ref[...]) * (1.0 - lam_init)).astype(BF16)
        else:
            o_ref[...] = jnp.where(lo, o0, o1).astype(BF16)

    def latent_queries():
        vals = [vc_ref[...], vl_ref[...]]
        finish(_softmax_pv(q0, [kc0[...], kl0[...]], vals), _softmax_pv(q1, [kc1[...], kl1[...]], vals))

    def context_queries():
        vals = [vc_ref[...]]
        finish(_softmax_pv(q0, [kc0[...]], vals), _softmax_pv(q1, [kc1[...]], vals))

    if want_ctx:
        t = pl.program_id(2)
        pl.when(t < nq)(latent_queries)
        pl.when(t >= nq)(context_queries)
    else:
        latent_queries()


def _q_row_map(rows, tq, want_ctx):
    nq = rows.S // tq
    nc = rows.C // tq
    B = rows.B

    def qrow(b, t):
        if not want_ctx:
            return b * nq + t
        return jnp.where(t < nq, b * nq + t, B * nq + b * nc + (t - nq))

    return qrow, nq, nc


def _attention(rows, tq, want_ctx, mode, q_src, k_src, v_src, n_groups, extra=(), lam_init=0.0, name="attn"):
    B, S, C = rows.B, rows.S, rows.C
    qrow, nq, nc = _q_row_map(rows, tq, want_ctx)
    n_q_tiles = nq + (nc if want_ctx else 0)
    ctx_blk0 = rows.n_lat // C
    split = len(q_src) == 1
    in_specs, args = [], []
    for arr, colf in q_src:
        in_specs.append(pl.BlockSpec((tq, LANES), lambda b, g, t, colf=colf: (qrow(b, t), colf(g))))
        args.append(arr)
    for arr, colf in k_src:
        in_specs.append(pl.BlockSpec((C, LANES), lambda b, g, t, colf=colf: (ctx_blk0 + b, colf(g))))
        args.append(arr)
    for arr, colf in k_src:
        in_specs.append(pl.BlockSpec((S, LANES), lambda b, g, t, colf=colf: (b, colf(g))))
        args.append(arr)
    varr, vcolf = v_src
    in_specs.append(pl.BlockSpec((C, LANES), lambda b, g, t: (ctx_blk0 + b, vcolf(g))))
    in_specs.append(pl.BlockSpec((S, LANES), lambda b, g, t: (b, vcolf(g))))
    args += [varr, varr]
    for e in extra:
        in_specs.append(pl.BlockSpec(e.shape, lambda b, g, t: (0, 0)))
        args.append(e)
    n_out_rows = rows.n_all if want_ctx else rows.n_lat
    return pl.pallas_call(
        functools.partial(_attn_kernel, mode=mode, split=split, nq=nq, want_ctx=want_ctx, lam_init=lam_init),
        grid=(B, n_groups, n_q_tiles),
        in_specs=in_specs,
        out_specs=pl.BlockSpec((tq, LANES), lambda b, g, t: (qrow(b, t), g)),
        out_shape=jax.ShapeDtypeStruct((n_out_rows, n_groups * LANES), BF16),
        compiler_params=_params(("parallel", "parallel", "arbitrary")),
        name=name,
    )(*args)


def _nbr_kernel(q_ref, kc_ref, kl_ref, vc_ref, vl_ref, bias_ref, o_ref, *, n_row_blocks, rows_per_step,
                grid_rows, want_ctx):
    lo64 = _lo_mask(GRID_W)
    win = NA_KH * GRID_W

    def latent_queries():
        tb = pl.program_id(2)
        kc, vc = kc_ref[...], vc_ref[...]
        for g in range(rows_per_step):
            r = tb * rows_per_step + g
            start = jnp.clip(r - NA_KH // 2, 0, grid_rows - NA_KH)
            di = start - r + (NA_KH - 1)
            off = pl.multiple_of(start * GRID_W, GRID_W)
            kw = kl_ref[pl.ds(off, win), :]
            vw = vl_ref[pl.ds(off, win), :]
            q = q_ref[g * GRID_W:(g + 1) * GRID_W, :]
            zero = jnp.zeros_like(q)
            o0 = _softmax_pv(jnp.where(lo64, q, zero), [kc, kw], [vc, vw], [None, bias_ref[0, di]])
            o1 = _softmax_pv(jnp.where(lo64, zero, q), [kc, kw], [vc, vw], [None, bias_ref[1, di]])
            o_ref[g * GRID_W:(g + 1) * GRID_W, :] = jnp.where(lo64, o0, o1).astype(BF16)

    def context_queries():
        q = q_ref[...]
        lo = _lo_mask(q.shape[0])
        zero = jnp.zeros_like(q)
        kc, vc = kc_ref[...], vc_ref[...]
        o_ref[...] = _pair_attention(jnp.where(lo, q, zero), jnp.where(lo, zero, q), [kc], [kc], [vc], lo).astype(BF16)

    if want_ctx:
        t = pl.program_id(2)
        pl.when(t < n_row_blocks)(latent_queries)
        pl.when(t >= n_row_blocks)(context_queries)
    else:
        latent_queries()


def _nbr_bias(rpb):
    kw = NA_KW
    col = jnp.arange(GRID_W)
    col_start = jnp.clip(col - kw // 2, 0, GRID_W - kw)
    in_win = (col[None, :] >= col_start[:, None]) & (col[None, :] < col_start[:, None] + kw)
    dc = jnp.clip(col[None, :] - col[:, None], -(kw - 1), kw - 1) + (NA_KW - 1)
    onehot = (dc[:, :, None] == jnp.arange(2 * NA_KW - 1)[None, None, :]).astype(F32)
    by_col = jnp.einsum("hrc,qkc->hrqk", rpb.astype(F32), onehot, precision=HIGHEST)
    bias = jnp.stack([by_col[:, di:di + NA_KH] for di in range(NA_KH)], axis=1)
    bias = jnp.where(in_win[None, None, :, None, :], bias.transpose(0, 1, 3, 2, 4), MASKED)
    return bias.reshape(rpb.shape[0], NA_KH, GRID_W, NA_KH * GRID_W)


def _nbr_attention(rows, want_ctx, misc, bias, q_blk0, k_blk0, v_blk0):
    B, S, C = rows.B, rows.S, rows.C
    tq = 256
    rows_per_step = tq // GRID_W
    grid_rows = S // GRID_W
    assert grid_rows >= NA_KH and C % tq == 0 and S % tq == 0
    qrow, nq, nc = _q_row_map(rows, tq, want_ctx)
    n_q_tiles = nq + (nc if want_ctx else 0)
    ctx_blk0 = rows.n_lat // C
    n_groups = B_HEADS // 2
    n_out_rows = rows.n_all if want_ctx else rows.n_lat
    return pl.pallas_call(
        functools.partial(_nbr_kernel, n_row_blocks=nq, rows_per_step=rows_per_step, grid_rows=grid_rows,
                          want_ctx=want_ctx),
        grid=(n_groups, B, n_q_tiles),
        in_specs=[pl.BlockSpec((tq, LANES), lambda g, b, t: (qrow(b, t), q_blk0 + g)),
                  pl.BlockSpec((C, LANES), lambda g, b, t: (ctx_blk0 + b, k_blk0 + g)),
                  pl.BlockSpec((S, LANES), lambda g, b, t: (b, k_blk0 + g)),
                  pl.BlockSpec((C, LANES), lambda g, b, t: (ctx_blk0 + b, v_blk0 + g)),
                  pl.BlockSpec((S, LANES), lambda g, b, t: (b, v_blk0 + g)),
                  pl.BlockSpec((2, NA_KH, GRID_W, NA_KH * GRID_W), lambda g, b, t: (g, 0, 0, 0))],
        out_specs=pl.BlockSpec((tq, LANES), lambda g, b, t: (qrow(b, t), g)),
        out_shape=jax.ShapeDtypeStruct((n_out_rows, n_groups * LANES), BF16),
        compiler_params=_params(("parallel", "parallel", "arbitrary")),
        name="attn_nbr",
    )(misc, misc, misc, misc, misc, bias)


def _residual_epilogue(x, y, gate, g_post, nxt):
    x_new = x + gate * _rms(y, g_post)
    h = None
    if nxt is not None:
        g_pre, sc, sh = nxt
        h = _modulated_norm(x_new, g_pre, sc, sh)
    return x_new, h


def _route(h, w_router):
    logits = lax.dot_general(h, w_router, (((1,), (0,)), ((), ())), precision=HIGHEST,
                             preferred_element_type=F32)
    n_e = logits.shape[1]
    idx = lax.broadcasted_iota(jnp.int32, logits.shape, 1)
    top1 = logits.max(axis=-1, keepdims=True)
    i1 = jnp.where(logits == top1, idx, n_e).min(axis=-1, keepdims=True)
    rest = jnp.where(idx == i1, -jnp.inf, logits)
    top2 = rest.max(axis=-1, keepdims=True)
    i2 = jnp.where(rest == top2, idx, n_e).min(axis=-1, keepdims=True)
    e2 = jnp.exp(top2 - top1)
    w1 = 1.0 / (1.0 + e2)
    w2 = e2 / (1.0 + e2)
    k = lax.broadcasted_iota(jnp.int32, (logits.shape[0], 2), 1)
    return jnp.where(k == 0, i1, i2), jnp.where(k == 0, w1, w2)


def _merge_kernel(gates_ref, oa_ref, ob_ref, oc_ref, od_ref, wbr_ref, wout_ref, x_ref, gate_ref, gpost_ref,
                  gpre_ref, sc_ref, sh_ref, *rest, routed):
    m = None
    for j, o_ref in enumerate((oa_ref, ob_ref, oc_ref, od_ref)):
        br = jnp.dot(o_ref[...], wbr_ref[j], preferred_element_type=F32)
        term = gates_ref[:, j * D_MODEL:(j + 1) * D_MODEL].astype(F32) * br
        m = term if m is None else m + term
    y = jnp.dot(m.astype(BF16), wout_ref[...], preferred_element_type=F32)
    x_new, h = _residual_epilogue(x_ref[...], y, gate_ref[0], gpost_ref[...],
                                  (gpre_ref[...], sc_ref[0], sh_ref[0]))
    if routed:
        wr_ref, x_out, h_out, idx_out, wts_out = rest
        idx_out[...], wts_out[...] = _route(h, wr_ref[...])
    else:
        x_out, h_out = rest
    x_out[...] = x_new
    h_out[...] = h.astype(h_out.dtype)


def _merge(rows, n_rows, gates, o_all, wbr, wout, x, mods, layer, g_post, g_pre, w_router, tm):
    row = lambda i: (i, 0)
    const2 = lambda i: (0, 0)
    routed = w_router is not None
    in_specs = [pl.BlockSpec((tm, N_BRANCH * D_MODEL), row)]
    in_specs += [pl.BlockSpec((tm, o.shape[1]), row) for o in o_all]
    in_specs += [pl.BlockSpec(wbr.shape, lambda i: (0, 0, 0)), pl.BlockSpec(wout.shape, const2),
                 pl.BlockSpec((tm, D_MODEL), row), rows.mod_spec(layer, 2, tm),
                 pl.BlockSpec((1, D_MODEL), const2), pl.BlockSpec((1, D_MODEL), const2),
                 rows.mod_spec(layer, 4, tm), rows.mod_spec(layer, 3, tm)]
    args = [gates, *o_all, wbr, wout, x, mods, g_post.reshape(1, -1), g_pre.reshape(1, -1), mods, mods]
    out_specs = [pl.BlockSpec((tm, D_MODEL), row), pl.BlockSpec((tm, D_MODEL), row)]
    out_shape = [jax.ShapeDtypeStruct((n_rows, D_MODEL), F32),
                 jax.ShapeDtypeStruct((n_rows, D_MODEL), F32 if routed else BF16)]
    if routed:
        in_specs.append(pl.BlockSpec(w_router.shape, const2))
        args.append(w_router)
        out_specs += [pl.BlockSpec((tm, 2), row), pl.BlockSpec((tm, 2), row)]
        out_shape += [jax.ShapeDtypeStruct((n_rows, 2), jnp.int32), jax.ShapeDtypeStruct((n_rows, 2), F32)]
    return pl.pallas_call(
        functools.partial(_merge_kernel, routed=routed),
        grid=(n_rows // tm,),
        in_specs=in_specs, out_specs=out_specs, out_shape=out_shape,
        compiler_params=_params(("parallel",)),
        name="merge_out",
    )(*args)


def _ffn_kernel(h_ref, w1_ref, w3_ref, w2_ref, x_ref, gate_ref, gpost_ref, *rest, has_next):
    f = pl.program_id(1)
    if has_next:
        gpre_ref, sc_ref, sh_ref, x_out, h_out, acc_ref = rest
    else:
        x_out, acc_ref = rest
    h = h_ref[...]
    a = jnp.dot(h, w1_ref[...], preferred_element_type=F32)
    b = jnp.dot(h, w3_ref[...], preferred_element_type=F32)
    part = jnp.dot((a * jax.nn.sigmoid(a) * b).astype(BF16), w2_ref[...], preferred_element_type=F32)

    @pl.when(f == 0)
    def _():
        acc_ref[...] = part

    @pl.when(f > 0)
    def _():
        acc_ref[...] += part

    @pl.when(f == pl.num_programs(1) - 1)
    def _():
        nxt = (gpre_ref[...], sc_ref[0], sh_ref[0]) if has_next else None
        x_new, hn = _residual_epilogue(x_ref[...], acc_ref[...], gate_ref[0], gpost_ref[...], nxt)
        x_out[...] = x_new
        if has_next:
            h_out[...] = hn.astype(BF16)


def _ffn(rows, n_rows, h, w1, w3, w2, x, mods, layer, g_post, g_pre_next, tm, tf):
    row = lambda i, f: (i, 0)
    const2 = lambda i, f: (0, 0)
    has_next = g_pre_next is not None
    ff = w1.shape[1]
    in_specs = [pl.BlockSpec((tm, D_MODEL), row),
                pl.BlockSpec((D_MODEL, tf), lambda i, f: (0, f)), pl.BlockSpec((D_MODEL, tf), lambda i, f: (0, f)),
                pl.BlockSpec((tf, D_MODEL), lambda i, f: (f, 0)),
                pl.BlockSpec((tm, D_MODEL), row), rows.mod_spec(layer, 5, tm), pl.BlockSpec((1, D_MODEL), const2)]
    args = [h, w1, w3, w2, x, mods, g_post.reshape(1, -1)]
    out_specs = [pl.BlockSpec((tm, D_MODEL), row)]
    out_shape = [jax.ShapeDtypeStruct((n_rows, D_MODEL), F32)]
    if has_next:
        in_specs += [pl.BlockSpec((1, D_MODEL), const2), rows.mod_spec(layer + 1, 1, tm),
                     rows.mod_spec(layer + 1, 0, tm)]
        args += [g_pre_next.reshape(1, -1), mods, mods]
        out_specs.append(pl.BlockSpec((tm, D_MODEL), row))
        out_shape.append(jax.ShapeDtypeStruct((n_rows, D_MODEL), BF16))
    return pl.pallas_call(
        functools.partial(_ffn_kernel, has_next=has_next),
        grid=(n_rows // tm, ff // tf),
        in_specs=in_specs, out_specs=out_specs, out_shape=out_shape,
        scratch_shapes=[pltpu.VMEM((tm, D_MODEL), F32)],
        compiler_params=_params(("parallel", "arbitrary")),
        name="ffn_dense",
    )(*args)


def _expert_slots(idx, n_tok, tme):
    n_e = N_EXPERTS
    n_asg = 2 * n_tok
    e_flat = idx.reshape(-1)
    onehot = (e_flat[:, None] == jnp.arange(n_e, dtype=jnp.int32)[None, :]).astype(jnp.int32)
    csum = jnp.cumsum(onehot, axis=0)
    counts = csum[-1]
    group = ((counts + tme - 1) // tme) * tme
    group_end = jnp.cumsum(group)
    group_start = group_end - group
    slot = jnp.sum(onehot * (csum - 1 + group_start[None, :]), axis=1)
    n_tiles = n_asg // tme + n_e
    token_of_slot = jnp.zeros((n_tiles * tme,), jnp.int32).at[slot].set(
        jnp.arange(n_asg, dtype=jnp.int32) // 2, unique_indices=True)
    tile_start = jnp.arange(n_tiles, dtype=jnp.int32) * tme
    tile_e = jnp.minimum(jnp.sum((tile_start[:, None] >= group_end[None, :]).astype(jnp.int32), axis=1), n_e - 1)
    sel = (tile_e[:, None] == jnp.arange(n_e, dtype=jnp.int32)[None, :]).astype(jnp.int32)
    filled_end = jnp.sum(sel * (group_start + counts)[None, :], axis=1)
    tile_rows = jnp.clip(filled_end - tile_start, 0, tme)
    return slot.astype(jnp.int32), token_of_slot, tile_e.astype(jnp.int32), tile_rows.astype(jnp.int32)


def _row_gather(src_hbm, dst, sem, index_of_row, n_rows):
    def body(r, carry):
        src_row = 0 if index_of_row is None else index_of_row(r)
        cp = pltpu.make_async_copy(src_hbm.at[pl.ds(src_row, 1), :], dst.at[pl.ds(r, 1), :], sem)
        if index_of_row is None:
            cp.wait()
        else:
            cp.start()
        return carry

    lax.fori_loop(0, n_rows, body, 0)


def _moe_ffn_kernel(tile_e, tile_rows, tok, h_hbm, w1_ref, w3_ref, w2_ref, y_ref, xg, xb, acc, sem, *, tme, sub):
    t = pl.program_id(0)
    f = pl.program_id(1)
    n_t = pl.num_programs(0)
    buf = t % 2

    def gather_rows(tile):
        return ((tile_rows[tile] + sub - 1) // sub) * sub

    def start(tile, b):
        _row_gather(h_hbm, xg.at[b], sem.at[b], lambda r: tok[tile * tme + r], gather_rows(tile))

    @pl.when(f == 0)
    def _():
        @pl.when(t == 0)
        def _():
            start(0, 0)

        @pl.when(t + 1 < n_t)
        def _():
            start(t + 1, 1 - buf)

        _row_gather(h_hbm, xg.at[buf], sem.at[buf], None, gather_rows(t))
        acc[...] = jnp.zeros_like(acc)

    w1 = w1_ref[0].astype(BF16)
    w3 = w3_ref[0].astype(BF16)
    w2 = w2_ref[0].astype(BF16)
    for sb in range(tme // sub):
        rs = slice(sb * sub, (sb + 1) * sub)

        @pl.when(sb * sub < tile_rows[t])
        def _():
            @pl.when(f == 0)
            def _():
                xb[rs, :] = xg[buf, rs, :].astype(BF16)

            x = xb[rs, :]
            a = jnp.dot(x, w1, preferred_element_type=F32)
            b = jnp.dot(x, w3, preferred_element_type=F32)
            acc[rs, :] += jnp.dot((a * jax.nn.sigmoid(a) * b).astype(BF16), w2, preferred_element_type=F32)

    @pl.when(f == pl.num_programs(1) - 1)
    def _():
        y_ref[...] = acc[...]


def _moe_ffn(h, token_of_slot, tile_e, tile_rows, w1, w3, w2, tme, tf):
    n_e, d, ff = w1.shape
    n_tiles = tile_e.shape[0]
    sub = 256
    grid_spec = pltpu.PrefetchScalarGridSpec(
        num_scalar_prefetch=3,
        grid=(n_tiles, ff // tf),
        in_specs=[pl.BlockSpec(memory_space=pl.ANY),
                  pl.BlockSpec((1, d, tf), lambda t, f, te, tr, tok: (te[t], 0, f)),
                  pl.BlockSpec((1, d, tf), lambda t, f, te, tr, tok: (te[t], 0, f)),
                  pl.BlockSpec((1, tf, d), lambda t, f, te, tr, tok: (te[t], f, 0))],
        out_specs=pl.BlockSpec((tme, d), lambda t, f, te, tr, tok: (t, 0)),
        scratch_shapes=[pltpu.VMEM((2, tme, d), F32), pltpu.VMEM((tme, d), BF16), pltpu.VMEM((tme, d), F32),
                        pltpu.SemaphoreType.DMA((2,))],
    )
    return pl.pallas_call(
        functools.partial(_moe_ffn_kernel, tme=tme, sub=sub),
        grid_spec=grid_spec,
        out_shape=jax.ShapeDtypeStruct((n_tiles * tme, d), F32),
        compiler_params=_params(("arbitrary", "arbitrary")),
        name="ffn_moe",
    )(tile_e, tile_rows, token_of_slot, h, w1, w3, w2)


def _moe_combine_kernel(slot, y_hbm, wts_ref, x_ref, gate_ref, gpost_ref, *rest, tm, has_next):
    if has_next:
        gpre_ref, sc_ref, sh_ref, x_out, h_out, yb, sem = rest
    else:
        x_out, yb, sem = rest
    i = pl.program_id(0)
    n_i = pl.num_programs(0)
    buf = i % 2

    def start(tile, b):
        for k in range(2):
            _row_gather(y_hbm, yb.at[b, k], sem.at[b], lambda r, k=k: slot[2 * (tile * tm + r) + k], tm)

    @pl.when(i == 0)
    def _():
        start(0, 0)

    @pl.when(i + 1 < n_i)
    def _():
        start(i + 1, 1 - buf)

    for k in range(2):
        _row_gather(y_hbm, yb.at[buf, k], sem.at[buf], None, tm)
    wts = wts_ref[...]
    y = wts[:, 0:1] * yb[buf, 0] + wts[:, 1:2] * yb[buf, 1]
    nxt = (gpre_ref[...], sc_ref[0], sh_ref[0]) if has_next else None
    x_new, hn = _residual_epilogue(x_ref[...], y, gate_ref[0], gpost_ref[...], nxt)
    x_out[...] = x_new
    if has_next:
        h_out[...] = hn.astype(BF16)


def _moe_combine(rows, n_rows, slot, y, wts, x, mods, layer, g_post, g_pre_next, tm):
    has_next = g_pre_next is not None
    row = lambda i, s: (i, 0)
    const2 = lambda i, s: (0, 0)

    mod_spec = lambda lyr, k: rows.mod_spec(lyr, k, tm)
    in_specs = [pl.BlockSpec(memory_space=pl.ANY), pl.BlockSpec((tm, 2), row),
                pl.BlockSpec((tm, D_MODEL), row), mod_spec(layer, 5), pl.BlockSpec((1, D_MODEL), const2)]
    args = [y, wts, x, mods, g_post.reshape(1, -1)]
    out_specs = [pl.BlockSpec((tm, D_MODEL), row)]
    out_shape = [jax.ShapeDtypeStruct((n_rows, D_MODEL), F32)]
    if has_next:
        in_specs += [pl.BlockSpec((1, D_MODEL), const2), mod_spec(layer + 1, 1), mod_spec(layer + 1, 0)]
        args += [g_pre_next.reshape(1, -1), mods, mods]
        out_specs.append(pl.BlockSpec((tm, D_MODEL), row))
        out_shape.append(jax.ShapeDtypeStruct((n_rows, D_MODEL), BF16))
    grid_spec = pltpu.PrefetchScalarGridSpec(
        num_scalar_prefetch=1, grid=(n_rows // tm,), in_specs=in_specs, out_specs=out_specs,
        scratch_shapes=[pltpu.VMEM((2, 2, tm, D_MODEL), F32), pltpu.SemaphoreType.DMA((2,))])
    return pl.pallas_call(
        functools.partial(_moe_combine_kernel, tm=tm, has_next=has_next),
        grid_spec=grid_spec, out_shape=out_shape,
        compiler_params=_params(("arbitrary",)),
        name="moe_combine",
    )(slot, *args)


def _in_proj_weights(w_in):
    widths = (512, 512, 512, 512, 512, 512, 512, 128, 128, MLA_Q_LORA, MLA_KV_LORA, MLA_ROPE, N_BRANCH * D_MODEL)
    cuts, o = [], 0
    for w in widths:
        cuts.append((o, o + w))
        o += w
    aq, ak, av, bq, bk, bv, cq, ck, cv, dqa, dkva, dkr, gates = [w_in[:, a:b] for a, b in cuts]
    cq = cq.reshape(-1, 2, 4, C_HEAD_DIM).transpose(0, 2, 1, 3).reshape(-1, C_HEADS * C_HEAD_DIM)
    dkr_pad = jnp.zeros((w_in.shape[0], LANES), w_in.dtype).at[:, MLA_NOPE:MLA_NOPE + MLA_ROPE].set(dkr)
    scale = A_QK_DIM ** -0.5
    w1 = jnp.concatenate([aq * scale, ak], axis=1)
    w2 = jnp.concatenate([av, bq * (B_HEAD_DIM ** -0.5), bk, bv], axis=1)
    w3 = jnp.concatenate([cq, ck, cv], axis=1)
    w5 = jnp.concatenate([dqa, dkva, dkr_pad], axis=1)
    return [w.astype(BF16) for w in (w1, w2, w3, gates, w5)]


def _mla_weights(w_uq, w_ukv):
    dq = MLA_NOPE + MLA_ROPE
    wuq = jnp.pad(w_uq.reshape(-1, D_HEADS, dq), ((0, 0), (0, 0), (0, LANES - dq))).reshape(-1, D_HEADS * LANES)
    kv = w_ukv.reshape(-1, D_HEADS, MLA_NOPE + MLA_V)
    wuk = jnp.pad(kv[:, :, :MLA_NOPE], ((0, 0), (0, 0), (0, LANES - MLA_NOPE))).reshape(-1, D_HEADS * LANES)
    wuv = kv[:, :, MLA_NOPE:].reshape(-1, D_HEADS * MLA_V)
    return wuq.astype(BF16), wuk.astype(BF16), wuv.astype(BF16)


def kernel(x, c, ctx, c_ctx, w_ada, b_ada, g_mix_pre, g_mix_post, g_ffn_pre, g_ffn_post, w_in, lam_q1, lam_k1, lam_q2, lam_k2, g_diff_sub, na_rpb, g_qnorm, g_knorm, g_q_lora, w_uq, g_kv_lora, w_ukv, w_br_a, w_br_b, w_br_c, w_br_d, w_out, w1_dense, w3_dense, w2_dense, w_router, w1_moe, w3_moe, w2_moe):
    B, S, D = x.shape
    C = ctx.shape[1]
    depth = w_in.shape[0]
    assert D == D_MODEL and B + 1 <= MOD_ROWS
    rows = _Rows(B, S, C)
    tq = 256
    tm_proj = _pick_tile((1024, 512, 256), S, B * C)
    tm_row = _pick_tile((512, 256), S, B * C)

    cvec = jnp.zeros((MOD_ROWS, D), F32).at[:B].set(c).at[B].set(c_ctx)
    mods = _mods(cvec, w_ada, b_ada)

    rope_qk = _rope_tables(S, A_QK_DIM, 0, tm_proj)
    rope_mla = _rope_tables(S, MLA_ROPE, MLA_NOPE, tm_proj)

    xs = jnp.concatenate([x.reshape(B * S, D), ctx.reshape(B * C, D)], axis=0)
    h = _prenorm(rows, xs, g_mix_pre[0], mods, 0, tm_proj)

    for i in range(depth):
        last = i == depth - 1
        want_ctx = not last
        n_rows = rows.n_all if want_ctx else rows.n_lat
        lam_init = 0.8 - 0.6 * math.exp(-0.3 * i)

        w1p, w2p, w3p, w4p, w5p = _in_proj_weights(w_in[i])
        qka = _proj(rows, h, w1p, tm_proj, 512, "rope", tables=rope_qk, name="proj_a_qk")
        misc = _proj(rows, h, w2p, tm_proj, 512, "plain", name="proj_misc")
        gain_c = jnp.concatenate([jnp.tile(g_qnorm[i], C_HEADS) * (C_HEAD_DIM ** -0.5),
                                  jnp.tile(g_knorm[i], C_KV_HEADS)]).reshape(1, -1).astype(F32)
        qkc = _proj(rows, h, w3p, tm_proj, w3p.shape[1], "headnorm_rope", tables=rope_qk, gain=gain_c,
                    name="proj_c")
        gates = _proj(rows, h, w4p, tm_proj, 512, "sigmoid", name="proj_gates")
        wuq, wuk, wuv = _mla_weights(w_uq[i], w_ukv[i])
        qd, kd, vd = _mla_proj(rows, h, w5p, rope_mla, g_q_lora[i], g_kv_lora[i], wuq, wuk, wuv, tm_row)

        lam_vecs = [v[i].reshape(1, -1).astype(F32) for v in (lam_q1, lam_k1, lam_q2, lam_k2)]
        oa = _attention(rows, tq, want_ctx, "diff",
                        q_src=[(qka, lambda g: g)], k_src=[(qka, lambda g: A_HEADS + g)],
                        v_src=(misc, lambda g: g), n_groups=A_HEADS,
                        extra=lam_vecs + [g_diff_sub[i].reshape(1, -1).astype(F32)], lam_init=lam_init,
                        name="attn_diff")
        ob = _nbr_attention(rows, want_ctx, misc, _nbr_bias(na_rpb[i]), q_blk0=4, k_blk0=8, v_blk0=12)
        oc = _attention(rows, tq, want_ctx, "select",
                        q_src=[(qkc, lambda g: g)], k_src=[(qkc, lambda g: 4)],
                        v_src=(qkc, lambda g: 5), n_groups=C_HEADS // 2, name="attn_gqa")
        od = _attention(rows, tq, want_ctx, "select",
                        q_src=[(qd, lambda g: 2 * g), (qd, lambda g: 2 * g + 1)],
                        k_src=[(kd, lambda g: 2 * g), (kd, lambda g: 2 * g + 1)],
                        v_src=(vd, lambda g: g), n_groups=D_HEADS // 2, name="attn_mla")

        wbc = w_br_c[i].reshape(2, 4, C_HEAD_DIM, D).transpose(1, 0, 2, 3).reshape(C_HEADS * C_HEAD_DIM, D)
        wbr = jnp.stack([w_br_a[i], w_br_b[i], wbc, w_br_d[i]]).astype(BF16)
        moe = i % 2 == 1
        j = i // 2
        merged = _merge(rows, n_rows, gates, (oa, ob, oc, od), wbr, w_out[i].astype(BF16), xs, mods, i,
                        g_mix_post[i], g_ffn_pre[i], w_router[j] if moe else None, tm_row)
        g_pre_next = None if last else g_mix_pre[i + 1]
        if moe:
            xs, h2, route_idx, route_wts = merged
            slot, token_of_slot, tile_e, tile_rows = _expert_slots(route_idx, n_rows, MOE_TILE)
            y = _moe_ffn(h2, token_of_slot, tile_e, tile_rows, w1_moe[j], w3_moe[j], w2_moe[j], MOE_TILE, 512)
            outs = _moe_combine(rows, n_rows, slot, y, route_wts, xs, mods, i, g_ffn_post[i], g_pre_next, 256)
        else:
            xs, h2 = merged
            outs = _ffn(rows, n_rows, h2, w1_dense[j].astype(BF16), w3_dense[j].astype(BF16),
                        w2_dense[j].astype(BF16), xs, mods, i, g_ffn_post[i], g_pre_next, tm_row, 1408)
        if last:
            xs = outs[0]
        else:
            xs, h = outs
    return xs[:B * S].reshape(B, S, D)
```

```python
import functools
import math

import jax
import jax.numpy as jnp
from jax import lax
from jax.experimental import pallas as pl
from jax.experimental.pallas import tpu as pltpu

F32 = jnp.float32
BF16 = jnp.bfloat16
HIGHEST = lax.Precision.HIGHEST

D_MODEL = 1024
GRID_W = 64
ROPE_BASE = 10000.0
EPS = 1e-6
N_BRANCH = 4
A_HEADS = 4
A_QK_DIM = 64
A_V_DIM = 128
B_HEADS = 8
B_HEAD_DIM = 64
NA_KH = 8
NA_KW = 16
C_HEADS = 8
C_KV_HEADS = 2
C_HEAD_DIM = 64
D_HEADS = 8
MLA_Q_LORA = 256
MLA_KV_LORA = 128
MLA_NOPE = 64
MLA_ROPE = 32
MLA_V = 64
N_EXPERTS = 8

LANES = 128
HALF = LANES // 2
MASKED = -1e30
LOG2E = math.log2(math.e)
PAIRS_PER_STEP = 4
N_MOD = 6
MOD_ROWS = 16
MOE_TILE = 1024
VMEM_LIMIT = 56 * 1024 * 1024

NT_DIMS = (((1,), (1,)), ((), ()))


def _pick_tile(candidates, *dims):
    for t in candidates:
        if all(d % t == 0 for d in dims):
            return t
    raise ValueError(f"no tile in {candidates} divides {dims}")


def _params(sem):
    return pltpu.CompilerParams(dimension_semantics=sem, vmem_limit_bytes=VMEM_LIMIT)


def _rms(x, g):
    return x * lax.rsqrt(jnp.mean(x * x, axis=-1, keepdims=True) + EPS) * g


def _modulated_norm(x, g_pre, sc, sh):
    return _rms(x, g_pre) * (1.0 + sc) + sh


def _rope(x, cos, sin_a, sin_b, half):
    up = pltpu.roll(x, LANES - half, axis=1)
    dn = pltpu.roll(x, half, axis=1)
    return x * cos + up * sin_a + dn * sin_b


def _group_mean_sq(x):
    r = lax.broadcasted_iota(jnp.int32, (LANES, LANES), 0) // HALF
    c = lax.broadcasted_iota(jnp.int32, (LANES, LANES), 1) // HALF
    ones = jnp.where(r == c, 1.0, 0.0).astype(F32)
    ss = lax.dot_general(x * x, ones, (((1,), (0,)), ((), ())), precision=HIGHEST,
                         preferred_element_type=F32)
    return ss * (1.0 / HALF)


def _rope_tables(seq_len, d_rot, lane_off, pad_rows):
    t = jnp.arange(seq_len)
    row = (t // GRID_W).astype(F32)
    col = (t % GRID_W).astype(F32)
    d_ax = d_rot // 2
    half = d_ax // 2
    inv = ROPE_BASE ** (-jnp.arange(0, d_ax, 2, dtype=F32) / d_ax)
    ang_r = row[:, None] * inv[None]
    ang_c = col[:, None] * inv[None]
    ang = jnp.concatenate([ang_r, ang_r, ang_c, ang_c], axis=-1)
    cos, sin = jnp.cos(ang), jnp.sin(ang)
    first = (jnp.arange(d_rot) % d_ax) < half
    sin_a = jnp.where(first[None], -sin, 0.0)
    sin_b = jnp.where(first[None], 0.0, sin)
    reps = (LANES - lane_off) // d_rot if lane_off == 0 else 1

    def place(tab, fill):
        tab = jnp.tile(tab, (1, reps))
        full = jnp.full((seq_len, LANES), fill, F32)
        full = full.at[:, lane_off:lane_off + tab.shape[1]].set(tab)
        return jnp.concatenate([full, jnp.full((pad_rows, LANES), fill, F32)], axis=0)

    return place(cos, 1.0), place(sin_a, 0.0), place(sin_b, 0.0), half


def _mods_kernel(c_ref, w_ref, b_ref, o_ref):
    c = c_ref[...]
    s = c * jax.nn.sigmoid(c)
    o_ref[0] = lax.dot_general(s, w_ref[0], (((1,), (0,)), ((), ())), precision=HIGHEST,
                               preferred_element_type=F32) + b_ref[0]


def _mods(cvec, w_ada, b_ada):
    n_layers, d, n6 = w_ada.shape
    tn = n6 // 4
    out = pl.pallas_call(
        _mods_kernel,
        grid=(n_layers, n6 // tn),
        in_specs=[pl.BlockSpec((MOD_ROWS, d), lambda l, j: (0, 0)),
                  pl.BlockSpec((1, d, tn), lambda l, j: (l, 0, j)),
                  pl.BlockSpec((1, 1, tn), lambda l, j: (l, 0, j))],
        out_specs=pl.BlockSpec((1, MOD_ROWS, tn), lambda l, j: (l, 0, j)),
        out_shape=jax.ShapeDtypeStruct((n_layers, MOD_ROWS, n6), F32),
        compiler_params=_params(("arbitrary", "arbitrary")),
        name="adaln_mods",
    )(cvec, w_ada, b_ada.reshape(n_layers, 1, n6))
    return out.reshape(n_layers * MOD_ROWS * N_MOD, 1, d)


class _Rows:
    def __init__(self, batch, seq, ctx_len):
        self.B, self.S, self.C = batch, seq, ctx_len
        self.n_lat = batch * seq
        self.n_all = batch * (seq + ctx_len)

    def mod_spec(self, layer, k, tm):
        n_lat_tiles = self.n_lat // tm
        per = self.S // tm
        B = self.B

        def imap(i, *_):
            brow = jnp.where(i < n_lat_tiles, i // per, B)
            return ((layer * MOD_ROWS + brow) * N_MOD + k, 0, 0)

        return pl.BlockSpec((1, 1, D_MODEL), imap)


def _prenorm_kernel(x_ref, g_ref, sc_ref, sh_ref, h_ref):
    h_ref[...] = _modulated_norm(x_ref[...], g_ref[...], sc_ref[0], sh_ref[0]).astype(BF16)


def _prenorm(rows, x, g_pre, mods, layer, tm):
    n = x.shape[0]
    row = lambda i: (i, 0)
    return pl.pallas_call(
        _prenorm_kernel,
        grid=(n // tm,),
        in_specs=[pl.BlockSpec((tm, D_MODEL), row),
                  pl.BlockSpec((1, D_MODEL), lambda i: (0, 0)),
                  rows.mod_spec(layer, 1, tm), rows.mod_spec(layer, 0, tm)],
        out_specs=pl.BlockSpec((tm, D_MODEL), row),
        out_shape=jax.ShapeDtypeStruct((n, D_MODEL), BF16),
        compiler_params=_params(("parallel",)),
        name="prenorm",
    )(x, g_pre.reshape(1, -1), mods, mods)


def _proj_kernel(h_ref, w_ref, *rest, epi, half):
    acc = jnp.dot(h_ref[...], w_ref[...], preferred_element_type=F32)
    o_ref = rest[-1]
    if epi == "plain":
        o_ref[...] = acc.astype(BF16)
    elif epi == "sigmoid":
        o_ref[...] = jax.nn.sigmoid(acc).astype(BF16)
    elif epi == "rope":
        cos_ref, sa_ref, sb_ref = rest[:3]
        for s in range(acc.shape[1] // LANES):
            sl = slice(s * LANES, (s + 1) * LANES)
            o_ref[:, sl] = _rope(acc[:, sl], cos_ref[...], sa_ref[...], sb_ref[...], half).astype(BF16)
    elif epi == "headnorm_rope":
        cos_ref, sa_ref, sb_ref, g_ref = rest[:4]
        n_norm = g_ref.shape[1] // LANES
        for s in range(acc.shape[1] // LANES):
            sl = slice(s * LANES, (s + 1) * LANES)
            blk = acc[:, sl]
            if s < n_norm:
                blk = blk * lax.rsqrt(_group_mean_sq(blk) + EPS) * g_ref[:, sl]
                blk = _rope(blk, cos_ref[...], sa_ref[...], sb_ref[...], half)
            o_ref[:, sl] = blk.astype(BF16)


def _rope_spec(rows, tm):
    n_lat_tiles = rows.n_lat // tm
    per = rows.S // tm
    return pl.BlockSpec((tm, LANES), lambda i, j: (jnp.where(i < n_lat_tiles, i % per, per), 0))


def _proj(rows, h, w, tm, tn, epi, tables=None, gain=None, name="proj"):
    n, k = h.shape
    nout = w.shape[1]
    in_specs = [pl.BlockSpec((tm, k), lambda i, j: (i, 0)),
                pl.BlockSpec((k, tn), lambda i, j: (0, j))]
    args = [h, w]
    half = 0
    if tables is not None:
        cos, sa, sb, half = tables
        in_specs += [_rope_spec(rows, tm)] * 3
        args += [cos, sa, sb]
    if gain is not None:
        in_specs.append(pl.BlockSpec((1, gain.shape[1]), lambda i, j: (0, 0)))
        args.append(gain)
    return pl.pallas_call(
        functools.partial(_proj_kernel, epi=epi, half=half),
        grid=(n // tm, nout // tn),
        in_specs=in_specs,
        out_specs=pl.BlockSpec((tm, tn), lambda i, j: (i, j)),
        out_shape=jax.ShapeDtypeStruct((n, nout), BF16),
        compiler_params=_params(("parallel", "arbitrary")),
        name=name,
    )(*args)


def _mla_kernel(h_ref, w_ref, cos_ref, sa_ref, sb_ref, gq_ref, gkv_ref, wuq_ref, wuk_ref, wuv_ref,
                qd_ref, kd_ref, vd_ref, *, half, scale):
    acc = jnp.dot(h_ref[...], w_ref[...], preferred_element_type=F32)
    cos, sa, sb = cos_ref[...], sa_ref[...], sb_ref[...]
    qn = _rms(acc[:, :MLA_Q_LORA], gq_ref[...]).astype(BF16)
    kvn = _rms(acc[:, MLA_Q_LORA:MLA_Q_LORA + MLA_KV_LORA], gkv_ref[...]).astype(BF16)
    k_rope = _rope(acc[:, MLA_Q_LORA + MLA_KV_LORA:], cos, sa, sb, half)
    q = jnp.dot(qn, wuq_ref[...], preferred_element_type=F32)
    k = jnp.dot(kvn, wuk_ref[...], preferred_element_type=F32)
    for s in range(D_HEADS):
        sl = slice(s * LANES, (s + 1) * LANES)
        qd_ref[:, sl] = (_rope(q[:, sl], cos, sa, sb, half) * scale).astype(BF16)
        kd_ref[:, sl] = (k[:, sl] + k_rope).astype(BF16)
    vd_ref[...] = jnp.dot(kvn, wuv_ref[...], preferred_element_type=F32).astype(BF16)


def _mla_proj(rows, h, w5, tables, g_q, g_kv, wuq, wuk, wuv, tm):
    n, k = h.shape
    cos, sa, sb, half = tables
    const = lambda i, j: (0, 0)
    nq, nv = D_HEADS * LANES, D_HEADS * MLA_V
    return pl.pallas_call(
        functools.partial(_mla_kernel, half=half, scale=(MLA_NOPE + MLA_ROPE) ** -0.5 * LOG2E),
        grid=(n // tm, 1),
        in_specs=[pl.BlockSpec((tm, k), lambda i, j: (i, 0)),
                  pl.BlockSpec(w5.shape, const),
                  _rope_spec(rows, tm), _rope_spec(rows, tm), _rope_spec(rows, tm),
                  pl.BlockSpec((1, MLA_Q_LORA), const), pl.BlockSpec((1, MLA_KV_LORA), const),
                  pl.BlockSpec(wuq.shape, const), pl.BlockSpec(wuk.shape, const), pl.BlockSpec(wuv.shape, const)],
        out_specs=[pl.BlockSpec((tm, nq), lambda i, j: (i, 0)),
                   pl.BlockSpec((tm, nq), lambda i, j: (i, 0)),
                   pl.BlockSpec((tm, nv), lambda i, j: (i, 0))],
        out_shape=[jax.ShapeDtypeStruct((n, nq), BF16), jax.ShapeDtypeStruct((n, nq), BF16),
                   jax.ShapeDtypeStruct((n, nv), BF16)],
        compiler_params=_params(("parallel", "arbitrary")),
        name="mla_proj",
    )(h, w5, cos, sa, sb, g_q.reshape(1, -1), g_kv.reshape(1, -1), wuq, wuk, wuv)


def _lo_rows(n):
    return lax.broadcasted_iota(jnp.int32, (LANES, n), 0) < HALF


def _stack_heads(q):
    half = q.shape[1] // 2
    lo = lax.broadcasted_iota(jnp.int32, q.shape, 1) < half
    zero = jnp.zeros_like(q)
    return jnp.concatenate([jnp.where(lo, q, zero), jnp.where(lo, zero, q)], axis=0)


def _transpose_bf16(v):
    return v.astype(F32).T.astype(BF16)


def _attend_t(problems):
    all_scores = []
    for w, keys, _, biases in problems:
        scores = []
        for idx, k in enumerate(keys):
            s = lax.dot_general(k, w, NT_DIMS, preferred_element_type=F32)
            if biases is not None and biases[idx] is not None:
                s = s + biases[idx]
            scores.append(s)
        all_scores.append(scores)
    outs = []
    for scores, (_, _, vals_t, _) in zip(all_scores, problems):
        m = scores[0].max(axis=0, keepdims=True)
        for s in scores[1:]:
            m = jnp.maximum(m, s.max(axis=0, keepdims=True))
        denom = None
        out = None
        for s, vt in zip(scores, vals_t):
            e = jnp.exp2(s - m)
            d = e.sum(axis=0, keepdims=True)
            o = jnp.dot(vt, e.astype(BF16), preferred_element_type=F32)
            denom = d if denom is None else denom + d
            out = o if out is None else out + o
        outs.append(out / denom)
    return outs


def _attn_kernel(q_ref, kc_ref, kl_ref, vc_ref, vl_ref, *rest, mode, nq, want_ctx, lam_init, n_sub, qk_lanes,
                 shared_kv):
    vtc, vtl = rest[-2:]
    o_ref = rest[-3]
    extra = rest[:-3]
    t = pl.program_id(2)
    tq = q_ref.shape[0]

    @pl.when(t == 0)
    def _():
        vtc[...] = _transpose_bf16(vc_ref[...])
        vtl[...] = _transpose_bf16(vl_ref[...])

    def finish(u, o_t):
        o0_t, o1_t = o_t[:, :tq], o_t[:, tq:]
        cols = slice(u * LANES, (u + 1) * LANES)
        if mode == "diff":
            lq1, lk1, lq2, lk2, g_ref = extra
            lam = (jnp.exp(jnp.sum(lq1[...] * lk1[...], axis=-1, keepdims=True))
                   - jnp.exp(jnp.sum(lq2[...] * lk2[...], axis=-1, keepdims=True)) + lam_init)
            o = (o0_t - lam * o1_t).T
            o_ref[:, cols] = (_rms(o, g_ref[...]) * (1.0 - lam_init)).astype(BF16)
        else:
            o_ref[:, cols] = jnp.where(_lo_rows(tq), o0_t, o1_t).T.astype(BF16)

    def run(with_latent):
        problems = []
        for u in range(n_sub):
            ql = slice(u * qk_lanes, (u + 1) * qk_lanes)
            kl = slice(0, qk_lanes) if shared_kv else ql
            vr = slice(0, LANES) if shared_kv else slice(u * LANES, (u + 1) * LANES)
            keys, vals = [kc_ref[:, kl]], [vtc[vr, :]]
            if with_latent:
                keys.append(kl_ref[:, kl])
                vals.append(vtl[vr, :])
            problems.append((_stack_heads(q_ref[:, ql]), keys, vals, None))
        for u, o_t in enumerate(_attend_t(problems)):
            finish(u, o_t)

    latent_queries = functools.partial(run, True)
    context_queries = functools.partial(run, False)

    if want_ctx:
        pl.when(t < nq)(latent_queries)
        pl.when(t >= nq)(context_queries)
    else:
        latent_queries()


def _q_row_map(rows, tq, want_ctx):
    nq = rows.S // tq
    nc = rows.C // tq
    B = rows.B

    def qrow(b, t):
        if not want_ctx:
            return b * nq + t
        return jnp.where(t < nq, b * nq + t, B * nq + b * nc + (t - nq))

    return qrow, nq, nc


def _attention(rows, tq, want_ctx, mode, q_src, k_src, v_src, n_groups, qk_lanes=LANES, shared_kv=False,
               extra=(), lam_init=0.0, name="attn"):
    B, S, C = rows.B, rows.S, rows.C
    n_sub = PAIRS_PER_STEP
    qrow, nq, nc = _q_row_map(rows, tq, want_ctx)
    n_q_tiles = nq + (nc if want_ctx else 0)
    ctx_blk0 = rows.n_lat // C
    (qarr, q0), (karr, k0), (varr, v0) = q_src, k_src, v_src
    qw = n_sub * qk_lanes
    kw = qk_lanes if shared_kv else qw
    vw = LANES if shared_kv else n_sub * LANES
    assert n_groups % n_sub == 0 and (q0 * qk_lanes) % qw == 0 and (k0 * qk_lanes) % kw == 0 and (v0 * LANES) % vw == 0
    qb, kb, vb = q0 * qk_lanes // qw, k0 * qk_lanes // kw, v0 * LANES // vw
    kstep = 0 if shared_kv else 1
    in_specs = [pl.BlockSpec((tq, qw), lambda b, g, t: (qrow(b, t), qb + g)),
                pl.BlockSpec((C, kw), lambda b, g, t: (ctx_blk0 + b, kb + kstep * g)),
                pl.BlockSpec((S, kw), lambda b, g, t: (b, kb + kstep * g)),
                pl.BlockSpec((C, vw), lambda b, g, t: (ctx_blk0 + b, vb + kstep * g)),
                pl.BlockSpec((S, vw), lambda b, g, t: (b, vb + kstep * g))]
    args = [qarr, karr, karr, varr, varr]
    for e in extra:
        in_specs.append(pl.BlockSpec(e.shape, lambda b, g, t: (0, 0)))
        args.append(e)
    n_out_rows = rows.n_all if want_ctx else rows.n_lat
    return pl.pallas_call(
        functools.partial(_attn_kernel, mode=mode, nq=nq, want_ctx=want_ctx, lam_init=lam_init, n_sub=n_sub,
                          qk_lanes=qk_lanes, shared_kv=shared_kv),
        grid=(B, n_groups // n_sub, n_q_tiles),
        in_specs=in_specs,
        out_specs=pl.BlockSpec((tq, n_sub * LANES), lambda b, g, t: (qrow(b, t), g)),
        out_shape=jax.ShapeDtypeStruct((n_out_rows, n_groups * LANES), BF16),
        scratch_shapes=[pltpu.VMEM((vw, C), BF16), pltpu.VMEM((vw, S), BF16)],
        compiler_params=_params(("parallel", "parallel", "arbitrary")),
        name=name,
    )(*args)


NBR_Q_ROWS = 4
NBR_WIN_ROWS = NBR_Q_ROWS + NA_KH


def _nbr_kernel(q_ref, kc_ref, kl_ref, vc_ref, vl_ref, bias_ref, o_ref, vtc, *, n_row_blocks, grid_rows, want_ctx):
    t = pl.program_id(2)
    tq = q_ref.shape[0]
    n_win = NBR_WIN_ROWS * GRID_W

    @pl.when(t == 0)
    def _():
        vtc[...] = _transpose_bf16(vc_ref[...])

    def finish(o_t):
        o_ref[...] = jnp.where(_lo_rows(tq), o_t[:, :tq], o_t[:, tq:]).T.astype(BF16)

    def latent_queries():
        base = jnp.clip(t * NBR_Q_ROWS - NA_KH // 2, 0, grid_rows - NBR_WIN_ROWS)
        off = pl.multiple_of(base * GRID_W, GRID_W)
        kw = kl_ref[pl.ds(off, n_win), :]
        vtw = _transpose_bf16(vl_ref[pl.ds(off, n_win), :])
        problem = (_stack_heads(q_ref[...]), [kc_ref[...], kw], [vtc[...], vtw], [None, bias_ref[0, 0]])
        finish(_attend_t([problem])[0])

    def context_queries():
        finish(_attend_t([(_stack_heads(q_ref[...]), [kc_ref[...]], [vtc[...]], None)])[0])

    if want_ctx:
        pl.when(t < n_row_blocks)(latent_queries)
        pl.when(t >= n_row_blocks)(context_queries)
    else:
        latent_queries()


def _nbr_bias(rpb):
    kw = NA_KW
    n_h = rpb.shape[0]
    col = jnp.arange(GRID_W)
    col_start = jnp.clip(col - kw // 2, 0, GRID_W - kw)
    in_win = (col[None, :] >= col_start[:, None]) & (col[None, :] < col_start[:, None] + kw)
    dc = jnp.clip(col[None, :] - col[:, None], -(kw - 1), kw - 1) + (NA_KW - 1)
    onehot = (dc[:, :, None] == jnp.arange(2 * NA_KW - 1)[None, None, :]).astype(F32)
    by_col = jnp.einsum("hrc,qkc->hrqk", rpb.astype(F32) * LOG2E, onehot, precision=HIGHEST)
    g = jnp.arange(NBR_Q_ROWS)[:, None]
    j = jnp.arange(NBR_WIN_ROWS)[None, :]
    first_row = jnp.stack([jnp.zeros_like(g), g, jnp.full_like(g, NBR_Q_ROWS)])
    rel_off = jnp.array([NA_KH - 1, NA_KH - 1 - NBR_Q_ROWS, NA_KH - 1 - 2 * NBR_Q_ROWS])[:, None, None]
    dr = j[None] - g[None] + rel_off
    row_ok = (j[None] >= first_row) & (j[None] < first_row + NA_KH)
    picked = jnp.take(by_col, jnp.clip(dr, 0, 2 * NA_KH - 2).reshape(-1), axis=1)
    picked = picked.reshape(n_h, 3, NBR_Q_ROWS, NBR_WIN_ROWS, GRID_W, GRID_W)
    ok = row_ok[None, :, :, :, None, None] & in_win[None, None, None, None, :, :]
    bias = jnp.where(ok, picked, MASKED).reshape(n_h // 2, 2, 3, NBR_Q_ROWS, NBR_WIN_ROWS, GRID_W, GRID_W)
    bias = bias.transpose(2, 0, 4, 6, 1, 3, 5)
    return bias.reshape(3, n_h // 2, NBR_WIN_ROWS * GRID_W, 2 * NBR_Q_ROWS * GRID_W)


def _nbr_attention(rows, want_ctx, misc, bias, q_blk0, k_blk0, v_blk0):
    B, S, C = rows.B, rows.S, rows.C
    tq = NBR_Q_ROWS * GRID_W
    grid_rows = S // GRID_W
    assert grid_rows >= NBR_WIN_ROWS and grid_rows % NBR_Q_ROWS == 0 and C % tq == 0
    qrow, nq, nc = _q_row_map(rows, tq, want_ctx)
    n_q_tiles = nq + (nc if want_ctx else 0)
    ctx_blk0 = rows.n_lat // C
    n_groups = B_HEADS // 2
    n_out_rows = rows.n_all if want_ctx else rows.n_lat
    n_keys = NBR_WIN_ROWS * GRID_W

    def bias_map(g, b, t):
        return (jnp.where(t == 0, 0, jnp.where(t < nq - 1, 1, 2)), g, 0, 0)

    return pl.pallas_call(
        functools.partial(_nbr_kernel, n_row_blocks=nq, grid_rows=grid_rows, want_ctx=want_ctx),
        grid=(n_groups, B, n_q_tiles),
        in_specs=[pl.BlockSpec((tq, LANES), lambda g, b, t: (qrow(b, t), q_blk0 + g)),
                  pl.BlockSpec((C, LANES), lambda g, b, t: (ctx_blk0 + b, k_blk0 + g)),
                  pl.BlockSpec((S, LANES), lambda g, b, t: (b, k_blk0 + g)),
                  pl.BlockSpec((C, LANES), lambda g, b, t: (ctx_blk0 + b, v_blk0 + g)),
                  pl.BlockSpec((S, LANES), lambda g, b, t: (b, v_blk0 + g)),
                  pl.BlockSpec((1, 1, n_keys, 2 * tq), bias_map)],
        out_specs=pl.BlockSpec((tq, LANES), lambda g, b, t: (qrow(b, t), g)),
        out_shape=jax.ShapeDtypeStruct((n_out_rows, n_groups * LANES), BF16),
        scratch_shapes=[pltpu.VMEM((LANES, C), BF16)],
        compiler_params=_params(("parallel", "parallel", "arbitrary")),
        name="attn_nbr",
    )(misc, misc, misc, misc, misc, bias)


def _residual_epilogue(x, y, gate, g_post, nxt):
    x_new = x + gate * _rms(y, g_post)
    h = None
    if nxt is not None:
        g_pre, sc, sh = nxt
        h = _modulated_norm(x_new, g_pre, sc, sh)
    return x_new, h


def _dot_split(a, b):
    a_hi, b_hi = a.astype(BF16), b.astype(BF16)
    a_lo = (a - a_hi.astype(F32)).astype(BF16)
    b_lo = (b - b_hi.astype(F32)).astype(BF16)
    dot = functools.partial(jnp.dot, preferred_element_type=F32)
    return dot(a_hi, b_hi) + (dot(a_hi, b_lo) + dot(a_lo, b_hi))


def _route(h, w_router):
    logits = _dot_split(h, w_router)
    n_e = logits.shape[1]
    idx = lax.broadcasted_iota(jnp.int32, logits.shape, 1)
    top1 = logits.max(axis=-1, keepdims=True)
    i1 = jnp.where(logits == top1, idx, n_e).min(axis=-1, keepdims=True)
    rest = jnp.where(idx == i1, -jnp.inf, logits)
    top2 = rest.max(axis=-1, keepdims=True)
    i2 = jnp.where(rest == top2, idx, n_e).min(axis=-1, keepdims=True)
    e2 = jnp.exp(top2 - top1)
    w1 = 1.0 / (1.0 + e2)
    w2 = e2 / (1.0 + e2)
    k = lax.broadcasted_iota(jnp.int32, (logits.shape[0], 2), 1)
    return jnp.where(k == 0, i1, i2), jnp.where(k == 0, w1, w2)


def _merge_kernel(gates_ref, oa_ref, ob_ref, oc_ref, od_ref, wbr_ref, wout_ref, x_ref, gate_ref, gpost_ref,
                  gpre_ref, sc_ref, sh_ref, *rest, routed):
    m = None
    for j, o_ref in enumerate((oa_ref, ob_ref, oc_ref, od_ref)):
        br = jnp.dot(o_ref[...], wbr_ref[j], preferred_element_type=F32)
        term = gates_ref[:, j * D_MODEL:(j + 1) * D_MODEL].astype(F32) * br
        m = term if m is None else m + term
    y = jnp.dot(m.astype(BF16), wout_ref[...], preferred_element_type=F32)
    x_new, h = _residual_epilogue(x_ref[...], y, gate_ref[0], gpost_ref[...],
                                  (gpre_ref[...], sc_ref[0], sh_ref[0]))
    if routed:
        wr_ref, x_out, h_out, idx_out, wts_out = rest
        idx_out[...], wts_out[...] = _route(h, wr_ref[...])
    else:
        x_out, h_out = rest
    x_out[...] = x_new
    h_out[...] = h.astype(h_out.dtype)


def _merge(rows, n_rows, gates, o_all, wbr, wout, x, mods, layer, g_post, g_pre, w_router, tm):
    row = lambda i: (i, 0)
    const2 = lambda i: (0, 0)
    routed = w_router is not None
    in_specs = [pl.BlockSpec((tm, N_BRANCH * D_MODEL), row)]
    in_specs += [pl.BlockSpec((tm, o.shape[1]), row) for o in o_all]
    in_specs += [pl.BlockSpec(wbr.shape, lambda i: (0, 0, 0)), pl.BlockSpec(wout.shape, const2),
                 pl.BlockSpec((tm, D_MODEL), row), rows.mod_spec(layer, 2, tm),
                 pl.BlockSpec((1, D_MODEL), const2), pl.BlockSpec((1, D_MODEL), const2),
                 rows.mod_spec(layer, 4, tm), rows.mod_spec(layer, 3, tm)]
    args = [gates, *o_all, wbr, wout, x, mods, g_post.reshape(1, -1), g_pre.reshape(1, -1), mods, mods]
    out_specs = [pl.BlockSpec((tm, D_MODEL), row), pl.BlockSpec((tm, D_MODEL), row)]
    out_shape = [jax.ShapeDtypeStruct((n_rows, D_MODEL), F32),
                 jax.ShapeDtypeStruct((n_rows, D_MODEL), F32 if routed else BF16)]
    if routed:
        in_specs.append(pl.BlockSpec(w_router.shape, const2))
        args.append(w_router)
        out_specs += [pl.BlockSpec((tm, 2), row), pl.BlockSpec((tm, 2), row)]
        out_shape += [jax.ShapeDtypeStruct((n_rows, 2), jnp.int32), jax.ShapeDtypeStruct((n_rows, 2), F32)]
    return pl.pallas_call(
        functools.partial(_merge_kernel, routed=routed),
        grid=(n_rows // tm,),
        in_specs=in_specs, out_specs=out_specs, out_shape=out_shape,
        compiler_params=_params(("parallel",)),
        name="merge_out",
    )(*args)


def _ffn_kernel(h_ref, w1_ref, w3_ref, w2_ref, x_ref, gate_ref, gpost_ref, *rest, has_next):
    f = pl.program_id(1)
    if has_next:
        gpre_ref, sc_ref, sh_ref, x_out, h_out, acc_ref = rest
    else:
        x_out, acc_ref = rest
    h = h_ref[...]
    a = jnp.dot(h, w1_ref[...], preferred_element_type=F32)
    b = jnp.dot(h, w3_ref[...], preferred_element_type=F32)
    part = jnp.dot((a * jax.nn.sigmoid(a) * b).astype(BF16), w2_ref[...], preferred_element_type=F32)

    @pl.when(f == 0)
    def _():
        acc_ref[...] = part

    @pl.when(f > 0)
    def _():
        acc_ref[...] += part

    @pl.when(f == pl.num_programs(1) - 1)
    def _():
        nxt = (gpre_ref[...], sc_ref[0], sh_ref[0]) if has_next else None
        x_new, hn = _residual_epilogue(x_ref[...], acc_ref[...], gate_ref[0], gpost_ref[...], nxt)
        x_out[...] = x_new
        if has_next:
            h_out[...] = hn.astype(BF16)


def _ffn(rows, n_rows, h, w1, w3, w2, x, mods, layer, g_post, g_pre_next, tm, tf):
    row = lambda i, f: (i, 0)
    const2 = lambda i, f: (0, 0)
    has_next = g_pre_next is not None
    ff = w1.shape[1]
    in_specs = [pl.BlockSpec((tm, D_MODEL), row),
                pl.BlockSpec((D_MODEL, tf), lambda i, f: (0, f)), pl.BlockSpec((D_MODEL, tf), lambda i, f: (0, f)),
                pl.BlockSpec((tf, D_MODEL), lambda i, f: (f, 0)),
                pl.BlockSpec((tm, D_MODEL), row), rows.mod_spec(layer, 5, tm), pl.BlockSpec((1, D_MODEL), const2)]
    args = [h, w1, w3, w2, x, mods, g_post.reshape(1, -1)]
    out_specs = [pl.BlockSpec((tm, D_MODEL), row)]
    out_shape = [jax.ShapeDtypeStruct((n_rows, D_MODEL), F32)]
    if has_next:
        in_specs += [pl.BlockSpec((1, D_MODEL), const2), rows.mod_spec(layer + 1, 1, tm),
                     rows.mod_spec(layer + 1, 0, tm)]
        args += [g_pre_next.reshape(1, -1), mods, mods]
        out_specs.append(pl.BlockSpec((tm, D_MODEL), row))
        out_shape.append(jax.ShapeDtypeStruct((n_rows, D_MODEL), BF16))
    return pl.pallas_call(
        functools.partial(_ffn_kernel, has_next=has_next),
        grid=(n_rows // tm, ff // tf),
        in_specs=in_specs, out_specs=out_specs, out_shape=out_shape,
        scratch_shapes=[pltpu.VMEM((tm, D_MODEL), F32)],
        compiler_params=_params(("parallel", "arbitrary")),
        name="ffn_dense",
    )(*args)


def _expert_slots(idx, n_tok, tme):
    n_e = N_EXPERTS
    n_asg = 2 * n_tok
    e_flat = idx.reshape(-1)
    onehot = (e_flat[:, None] == jnp.arange(n_e, dtype=jnp.int32)[None, :]).astype(jnp.int32)
    csum = jnp.cumsum(onehot, axis=0)
    counts = csum[-1]
    group = ((counts + tme - 1) // tme) * tme
    group_end = jnp.cumsum(group)
    group_start = group_end - group
    slot = jnp.sum(onehot * (csum - 1 + group_start[None, :]), axis=1)
    n_tiles = n_asg // tme + n_e
    token_of_slot = jnp.zeros((n_tiles * tme,), jnp.int32).at[slot].set(
        jnp.arange(n_asg, dtype=jnp.int32) // 2, unique_indices=True)
    tile_start = jnp.arange(n_tiles, dtype=jnp.int32) * tme
    tile_e = jnp.minimum(jnp.sum((tile_start[:, None] >= group_end[None, :]).astype(jnp.int32), axis=1), n_e - 1)
    sel = (tile_e[:, None] == jnp.arange(n_e, dtype=jnp.int32)[None, :]).astype(jnp.int32)
    filled_end = jnp.sum(sel * (group_start + counts)[None, :], axis=1)
    tile_rows = jnp.clip(filled_end - tile_start, 0, tme)
    return slot.astype(jnp.int32), token_of_slot, tile_e.astype(jnp.int32), tile_rows.astype(jnp.int32)


GATHER_BLOCK = 256
GATHER_UNROLL = 8


def _row_gather(src_hbm, dst, sem, index_of_row, n_blocks):
    def block(blk, carry):
        def body(i, c):
            r = blk * GATHER_BLOCK + i
            src_row = 0 if index_of_row is None else index_of_row(r)
            cp = pltpu.make_async_copy(src_hbm.at[pl.ds(src_row, 1), :], dst.at[pl.ds(r, 1), :], sem)
            if index_of_row is None:
                cp.wait()
            else:
                cp.start()
            return c

        return lax.fori_loop(0, GATHER_BLOCK, body, carry, unroll=GATHER_UNROLL)

    lax.fori_loop(0, n_blocks, block, 0)


def _moe_ffn_kernel(tile_e, tile_rows, tok, h_hbm, w1_ref, w3_ref, w2_ref, y_ref, xg, xb, acc, sem, *, tme, sub):
    t = pl.program_id(0)
    f = pl.program_id(1)
    n_t = pl.num_programs(0)
    buf = t % 2

    def gather_blocks(tile):
        return (tile_rows[tile] + sub - 1) // sub

    def start(tile, b):
        _row_gather(h_hbm, xg.at[b], sem.at[b], lambda r: tok[tile * tme + r], gather_blocks(tile))

    @pl.when(f == 0)
    def _():
        @pl.when(t == 0)
        def _():
            start(0, 0)

        @pl.when(t + 1 < n_t)
        def _():
            start(t + 1, 1 - buf)

        _row_gather(h_hbm, xg.at[buf], sem.at[buf], None, gather_blocks(t))
        acc[...] = jnp.zeros_like(acc)

    w1 = w1_ref[0].astype(BF16)
    w3 = w3_ref[0].astype(BF16)
    w2 = w2_ref[0].astype(BF16)
    for sb in range(tme // sub):
        rs = slice(sb * sub, (sb + 1) * sub)

        @pl.when(sb * sub < tile_rows[t])
        def _():
            @pl.when(f == 0)
            def _():
                xb[rs, :] = xg[buf, rs, :].astype(BF16)

            x = xb[rs, :]
            a = jnp.dot(x, w1, preferred_element_type=F32)
            b = jnp.dot(x, w3, preferred_element_type=F32)
            acc[rs, :] += jnp.dot((a * jax.nn.sigmoid(a) * b).astype(BF16), w2, preferred_element_type=F32)

    @pl.when(f == pl.num_programs(1) - 1)
    def _():
        y_ref[...] = acc[...]


def _moe_ffn(h, token_of_slot, tile_e, tile_rows, w1, w3, w2, tme, tf):
    n_e, d, ff = w1.shape
    n_tiles = tile_e.shape[0]
    sub = GATHER_BLOCK
    grid_spec = pltpu.PrefetchScalarGridSpec(
        num_scalar_prefetch=3,
        grid=(n_tiles, ff // tf),
        in_specs=[pl.BlockSpec(memory_space=pl.ANY),
                  pl.BlockSpec((1, d, tf), lambda t, f, te, tr, tok: (te[t], 0, f)),
                  pl.BlockSpec((1, d, tf), lambda t, f, te, tr, tok: (te[t], 0, f)),
                  pl.BlockSpec((1, tf, d), lambda t, f, te, tr, tok: (te[t], f, 0))],
        out_specs=pl.BlockSpec((tme, d), lambda t, f, te, tr, tok: (t, 0)),
        scratch_shapes=[pltpu.VMEM((2, tme, d), F32), pltpu.VMEM((tme, d), BF16), pltpu.VMEM((tme, d), F32),
                        pltpu.SemaphoreType.DMA((2,))],
    )
    return pl.pallas_call(
        functools.partial(_moe_ffn_kernel, tme=tme, sub=sub),
        grid_spec=grid_spec,
        out_shape=jax.ShapeDtypeStruct((n_tiles * tme, d), F32),
        compiler_params=_params(("arbitrary", "arbitrary")),
        name="ffn_moe",
    )(tile_e, tile_rows, token_of_slot, h, w1, w3, w2)


def _moe_combine_kernel(slot, y_hbm, wts_ref, x_ref, gate_ref, gpost_ref, *rest, tm, has_next):
    if has_next:
        gpre_ref, sc_ref, sh_ref, x_out, h_out, yb, sem = rest
    else:
        x_out, yb, sem = rest
    i = pl.program_id(0)
    n_i = pl.num_programs(0)
    buf = i % 2

    def start(tile, b):
        for k in range(2):
            _row_gather(y_hbm, yb.at[b, k], sem.at[b], lambda r, k=k: slot[2 * (tile * tm + r) + k], tm // GATHER_BLOCK)

    @pl.when(i == 0)
    def _():
        start(0, 0)

    @pl.when(i + 1 < n_i)
    def _():
        start(i + 1, 1 - buf)

    for k in range(2):
        _row_gather(y_hbm, yb.at[buf, k], sem.at[buf], None, tm // GATHER_BLOCK)
    wts = wts_ref[...]
    y = wts[:, 0:1] * yb[buf, 0] + wts[:, 1:2] * yb[buf, 1]
    nxt = (gpre_ref[...], sc_ref[0], sh_ref[0]) if has_next else None
    x_new, hn = _residual_epilogue(x_ref[...], y, gate_ref[0], gpost_ref[...], nxt)
    x_out[...] = x_new
    if has_next:
        h_out[...] = hn.astype(BF16)


def _moe_combine(rows, n_rows, slot, y, wts, x, mods, layer, g_post, g_pre_next, tm):
    has_next = g_pre_next is not None
    row = lambda i, s: (i, 0)
    const2 = lambda i, s: (0, 0)

    mod_spec = lambda lyr, k: rows.mod_spec(lyr, k, tm)
    in_specs = [pl.BlockSpec(memory_space=pl.ANY), pl.BlockSpec((tm, 2), row),
                pl.BlockSpec((tm, D_MODEL), row), mod_spec(layer, 5), pl.BlockSpec((1, D_MODEL), const2)]
    args = [y, wts, x, mods, g_post.reshape(1, -1)]
    out_specs = [pl.BlockSpec((tm, D_MODEL), row)]
    out_shape = [jax.ShapeDtypeStruct((n_rows, D_MODEL), F32)]
    if has_next:
        in_specs += [pl.BlockSpec((1, D_MODEL), const2), mod_spec(layer + 1, 1), mod_spec(layer + 1, 0)]
        args += [g_pre_next.reshape(1, -1), mods, mods]
        out_specs.append(pl.BlockSpec((tm, D_MODEL), row))
        out_shape.append(jax.ShapeDtypeStruct((n_rows, D_MODEL), BF16))
    grid_spec = pltpu.PrefetchScalarGridSpec(
        num_scalar_prefetch=1, grid=(n_rows // tm,), in_specs=in_specs, out_specs=out_specs,
        scratch_shapes=[pltpu.VMEM((2, 2, tm, D_MODEL), F32), pltpu.SemaphoreType.DMA((2,))])
    return pl.pallas_call(
        functools.partial(_moe_combine_kernel, tm=tm, has_next=has_next),
        grid_spec=grid_spec, out_shape=out_shape,
        compiler_params=_params(("arbitrary",)),
        name="moe_combine",
    )(slot, *args)


def _in_proj_weights(w_in):
    widths = (512, 512, 512, 512, 512, 512, 512, 128, 128, MLA_Q_LORA, MLA_KV_LORA, MLA_ROPE, N_BRANCH * D_MODEL)
    cuts, o = [], 0
    for w in widths:
        cuts.append((o, o + w))
        o += w
    aq, ak, av, bq, bk, bv, cq, ck, cv, dqa, dkva, dkr, gates = [w_in[:, a:b] for a, b in cuts]
    cq = cq.reshape(-1, 2, 4, C_HEAD_DIM).transpose(0, 2, 1, 3).reshape(-1, C_HEADS * C_HEAD_DIM)
    dkr_pad = jnp.zeros((w_in.shape[0], LANES), w_in.dtype).at[:, MLA_NOPE:MLA_NOPE + MLA_ROPE].set(dkr)
    w1 = jnp.concatenate([aq * (A_QK_DIM ** -0.5 * LOG2E), ak], axis=1)
    w2 = jnp.concatenate([av, bq * (B_HEAD_DIM ** -0.5 * LOG2E), bk, bv], axis=1)
    w3 = jnp.concatenate([cq, ck, cv], axis=1)
    w5 = jnp.concatenate([dqa, dkva, dkr_pad], axis=1)
    return [w.astype(BF16) for w in (w1, w2, w3, gates, w5)]


def _mla_weights(w_uq, w_ukv):
    dq = MLA_NOPE + MLA_ROPE
    wuq = jnp.pad(w_uq.reshape(-1, D_HEADS, dq), ((0, 0), (0, 0), (0, LANES - dq))).reshape(-1, D_HEADS * LANES)
    kv = w_ukv.reshape(-1, D_HEADS, MLA_NOPE + MLA_V)
    wuk = jnp.pad(kv[:, :, :MLA_NOPE], ((0, 0), (0, 0), (0, LANES - MLA_NOPE))).reshape(-1, D_HEADS * LANES)
    wuv = kv[:, :, MLA_NOPE:].reshape(-1, D_HEADS * MLA_V)
    return wuq.astype(BF16), wuk.astype(BF16), wuv.astype(BF16)


def kernel(x, c, ctx, c_ctx, w_ada, b_ada, g_mix_pre, g_mix_post, g_ffn_pre, g_ffn_post, w_in, lam_q1, lam_k1, lam_q2, lam_k2, g_diff_sub, na_rpb, g_qnorm, g_knorm, g_q_lora, w_uq, g_kv_lora, w_ukv, w_br_a, w_br_b, w_br_c, w_br_d, w_out, w1_dense, w3_dense, w2_dense, w_router, w1_moe, w3_moe, w2_moe):
    B, S, D = x.shape
    C = ctx.shape[1]
    depth = w_in.shape[0]
    assert D == D_MODEL and B + 1 <= MOD_ROWS
    rows = _Rows(B, S, C)
    tq = 256
    tm_proj = _pick_tile((1024, 512, 256), S, B * C)
    tm_row = _pick_tile((512, 256), S, B * C)

    cvec = jnp.zeros((MOD_ROWS, D), F32).at[:B].set(c).at[B].set(c_ctx)
    mods = _mods(cvec, w_ada, b_ada)

    rope_qk = _rope_tables(S, A_QK_DIM, 0, tm_proj)
    rope_mla = _rope_tables(S, MLA_ROPE, MLA_NOPE, tm_proj)

    xs = jnp.concatenate([x.reshape(B * S, D), ctx.reshape(B * C, D)], axis=0)
    h = _prenorm(rows, xs, g_mix_pre[0], mods, 0, tm_proj)

    for i in range(depth):
        last = i == depth - 1
        want_ctx = not last
        n_rows = rows.n_all if want_ctx else rows.n_lat
        lam_init = 0.8 - 0.6 * math.exp(-0.3 * i)

        w1p, w2p, w3p, w4p, w5p = _in_proj_weights(w_in[i])
        qka = _proj(rows, h, w1p, tm_proj, 512, "rope", tables=rope_qk, name="proj_a_qk")
        misc = _proj(rows, h, w2p, tm_proj, 512, "plain", name="proj_misc")
        gain_c = jnp.concatenate([jnp.tile(g_qnorm[i], C_HEADS) * (C_HEAD_DIM ** -0.5 * LOG2E),
                                  jnp.tile(g_knorm[i], C_KV_HEADS)]).reshape(1, -1).astype(F32)
        qkc = _proj(rows, h, w3p, tm_proj, w3p.shape[1], "headnorm_rope", tables=rope_qk, gain=gain_c,
                    name="proj_c")
        gates = _proj(rows, h, w4p, tm_proj, 512, "sigmoid", name="proj_gates")
        wuq, wuk, wuv = _mla_weights(w_uq[i], w_ukv[i])
        qd, kd, vd = _mla_proj(rows, h, w5p, rope_mla, g_q_lora[i], g_kv_lora[i], wuq, wuk, wuv, tm_row)

        lam_vecs = [v[i].reshape(1, -1).astype(F32) for v in (lam_q1, lam_k1, lam_q2, lam_k2)]
        oa = _attention(rows, tq, want_ctx, "diff",
                        q_src=(qka, 0), k_src=(qka, A_HEADS), v_src=(misc, 0), n_groups=A_HEADS,
                        extra=lam_vecs + [g_diff_sub[i].reshape(1, -1).astype(F32)], lam_init=lam_init,
                        name="attn_diff")
        ob = _nbr_attention(rows, want_ctx, misc, _nbr_bias(na_rpb[i]), q_blk0=4, k_blk0=8, v_blk0=12)
        oc = _attention(rows, tq, want_ctx, "select",
                        q_src=(qkc, 0), k_src=(qkc, 4), v_src=(qkc, 5), n_groups=C_HEADS // 2, shared_kv=True,
                        name="attn_gqa")
        od = _attention(rows, tq, want_ctx, "select",
                        q_src=(qd, 0), k_src=(kd, 0), v_src=(vd, 0),
                        n_groups=D_HEADS // 2, qk_lanes=2 * LANES, name="attn_mla")

        wbc = w_br_c[i].reshape(2, 4, C_HEAD_DIM, D).transpose(1, 0, 2, 3).reshape(C_HEADS * C_HEAD_DIM, D)
        wbr = jnp.stack([w_br_a[i], w_br_b[i], wbc, w_br_d[i]]).astype(BF16)
        moe = i % 2 == 1
        j = i // 2
        merged = _merge(rows, n_rows, gates, (oa, ob, oc, od), wbr, w_out[i].astype(BF16), xs, mods, i,
                        g_mix_post[i], g_ffn_pre[i], w_router[j] if moe else None, tm_row)
        g_pre_next = None if last else g_mix_pre[i + 1]
        if moe:
            xs, h2, route_idx, route_wts = merged
            slot, token_of_slot, tile_e, tile_rows = _expert_slots(route_idx, n_rows, MOE_TILE)
            y = _moe_ffn(h2, token_of_slot, tile_e, tile_rows, w1_moe[j], w3_moe[j], w2_moe[j], MOE_TILE, 512)
            outs = _moe_combine(rows, n_rows, slot, y, route_wts, xs, mods, i, g_ffn_post[i], g_pre_next, 256)
        else:
            xs, h2 = merged
            outs = _ffn(rows, n_rows, h2, w1_dense[j].astype(BF16), w3_dense[j].astype(BF16),
                        w2_dense[j].astype(BF16), xs, mods, i, g_ffn_post[i], g_pre_next, tm_row, 1408)
        if last:
            xs = outs[0]
        else:
            xs, h = outs
    return xs[:B * S].reshape(B, S, D)
```

```python
import functools
import math

import jax
import jax.numpy as jnp
from jax import lax
from jax.experimental import pallas as pl
from jax.experimental.pallas import tpu as pltpu

F32 = jnp.float32
BF16 = jnp.bfloat16
HIGHEST = lax.Precision.HIGHEST

D_MODEL = 1024
GRID_W = 64
ROPE_BASE = 10000.0
EPS = 1e-6
N_BRANCH = 4
A_HEADS = 4
A_QK_DIM = 64
A_V_DIM = 128
B_HEADS = 8
B_HEAD_DIM = 64
NA_KH = 8
NA_KW = 16
C_HEADS = 8
C_KV_HEADS = 2
C_HEAD_DIM = 64
D_HEADS = 8
MLA_Q_LORA = 256
MLA_KV_LORA = 128
MLA_NOPE = 64
MLA_ROPE = 32
MLA_V = 64
N_EXPERTS = 8

LANES = 128
HALF = LANES // 2
MASKED = -1e30
LOG2E = math.log2(math.e)
PAIRS_PER_STEP = 4
N_MOD = 6
MOD_ROWS = 16
MOE_TILE = 1024
VMEM_LIMIT = 56 * 1024 * 1024

NT_DIMS = (((1,), (1,)), ((), ()))


def _pick_tile(candidates, *dims):
    for t in candidates:
        if all(d % t == 0 for d in dims):
            return t
    raise ValueError(f"no tile in {candidates} divides {dims}")


def _params(sem):
    return pltpu.CompilerParams(dimension_semantics=sem, vmem_limit_bytes=VMEM_LIMIT)


def _rms(x, g):
    return x * lax.rsqrt(jnp.mean(x * x, axis=-1, keepdims=True) + EPS) * g


def _modulated_norm(x, g_pre, sc, sh):
    return _rms(x, g_pre) * (1.0 + sc) + sh


def _rope(x, cos, sin_a, sin_b, half):
    up = pltpu.roll(x, LANES - half, axis=1)
    dn = pltpu.roll(x, half, axis=1)
    return x * cos + up * sin_a + dn * sin_b


def _group_mean_sq(x):
    r = lax.broadcasted_iota(jnp.int32, (LANES, LANES), 0) // HALF
    c = lax.broadcasted_iota(jnp.int32, (LANES, LANES), 1) // HALF
    ones = jnp.where(r == c, 1.0, 0.0).astype(BF16)
    sq = x * x
    hi = sq.astype(BF16)
    lo = (sq - hi.astype(F32)).astype(BF16)
    ss = jnp.dot(hi, ones, preferred_element_type=F32) + jnp.dot(lo, ones, preferred_element_type=F32)
    return ss * (1.0 / HALF)


def _rope_tables(seq_len, d_rot, lane_off, pad_rows):
    t = jnp.arange(seq_len)
    row = (t // GRID_W).astype(F32)
    col = (t % GRID_W).astype(F32)
    d_ax = d_rot // 2
    half = d_ax // 2
    inv = ROPE_BASE ** (-jnp.arange(0, d_ax, 2, dtype=F32) / d_ax)
    ang_r = row[:, None] * inv[None]
    ang_c = col[:, None] * inv[None]
    ang = jnp.concatenate([ang_r, ang_r, ang_c, ang_c], axis=-1)
    cos, sin = jnp.cos(ang), jnp.sin(ang)
    first = (jnp.arange(d_rot) % d_ax) < half
    sin_a = jnp.where(first[None], -sin, 0.0)
    sin_b = jnp.where(first[None], 0.0, sin)
    reps = (LANES - lane_off) // d_rot if lane_off == 0 else 1

    def place(tab, fill):
        tab = jnp.tile(tab, (1, reps))
        full = jnp.full((seq_len, LANES), fill, F32)
        full = full.at[:, lane_off:lane_off + tab.shape[1]].set(tab)
        return jnp.concatenate([full, jnp.full((pad_rows, LANES), fill, F32)], axis=0)

    return place(cos, 1.0), place(sin_a, 0.0), place(sin_b, 0.0), half


def _mods_kernel(c_ref, w_ref, b_ref, o_ref):
    c = c_ref[...]
    s = c * jax.nn.sigmoid(c)
    o_ref[0] = lax.dot_general(s, w_ref[0], (((1,), (0,)), ((), ())), precision=HIGHEST,
                               preferred_element_type=F32) + b_ref[0]


def _mods(cvec, w_ada, b_ada):
    n_layers, d, n6 = w_ada.shape
    tn = n6 // 4
    out = pl.pallas_call(
        _mods_kernel,
        grid=(n_layers, n6 // tn),
        in_specs=[pl.BlockSpec((MOD_ROWS, d), lambda l, j: (0, 0)),
                  pl.BlockSpec((1, d, tn), lambda l, j: (l, 0, j)),
                  pl.BlockSpec((1, 1, tn), lambda l, j: (l, 0, j))],
        out_specs=pl.BlockSpec((1, MOD_ROWS, tn), lambda l, j: (l, 0, j)),
        out_shape=jax.ShapeDtypeStruct((n_layers, MOD_ROWS, n6), F32),
        compiler_params=_params(("arbitrary", "arbitrary")),
        name="adaln_mods",
    )(cvec, w_ada, b_ada.reshape(n_layers, 1, n6))
    return out.reshape(n_layers * MOD_ROWS * N_MOD, 1, d)


class _Rows:
    def __init__(self, batch, seq, ctx_len):
        self.B, self.S, self.C = batch, seq, ctx_len
        self.n_lat = batch * seq
        self.n_all = batch * (seq + ctx_len)

    def mod_spec(self, layer, k, tm):
        n_lat_tiles = self.n_lat // tm
        per = self.S // tm
        B = self.B

        def imap(i, *_):
            brow = jnp.where(i < n_lat_tiles, i // per, B)
            return ((layer * MOD_ROWS + brow) * N_MOD + k, 0, 0)

        return pl.BlockSpec((1, 1, D_MODEL), imap)


def _prenorm_kernel(x_ref, g_ref, sc_ref, sh_ref, h_ref):
    h_ref[...] = _modulated_norm(x_ref[...], g_ref[...], sc_ref[0], sh_ref[0]).astype(BF16)


def _prenorm(rows, x, g_pre, mods, layer, tm):
    n = x.shape[0]
    row = lambda i: (i, 0)
    return pl.pallas_call(
        _prenorm_kernel,
        grid=(n // tm,),
        in_specs=[pl.BlockSpec((tm, D_MODEL), row),
                  pl.BlockSpec((1, D_MODEL), lambda i: (0, 0)),
                  rows.mod_spec(layer, 1, tm), rows.mod_spec(layer, 0, tm)],
        out_specs=pl.BlockSpec((tm, D_MODEL), row),
        out_shape=jax.ShapeDtypeStruct((n, D_MODEL), BF16),
        compiler_params=_params(("parallel",)),
        name="prenorm",
    )(x, g_pre.reshape(1, -1), mods, mods)


PROJ_CHUNK = 512


def _proj_chunks(x, w_ref, o_ref, epilogue):
    n = w_ref.shape[1]
    for c0 in range(0, n, PROJ_CHUNK):
        c1 = min(c0 + PROJ_CHUNK, n)
        acc = jnp.dot(x, w_ref[:, c0:c1], preferred_element_type=F32)
        for s in range((c1 - c0) // LANES):
            col = c0 + s * LANES
            o_ref[:, col:col + LANES] = epilogue(acc[:, s * LANES:(s + 1) * LANES], col).astype(BF16)


def _in_proj_kernel(h_ref, w1_ref, w2_ref, w3_ref, w4_ref, w5_ref, wuq_ref, wuk_ref, wuv_ref,
                    cos_ref, sa_ref, sb_ref, cosd_ref, sad_ref, sbd_ref, gc_ref, gq_ref, gkv_ref,
                    qka_ref, misc_ref, qkc_ref, gates_ref, qd_ref, kd_ref, vd_ref, *, half_qk, half_mla, mla_scale):
    h = h_ref[...]
    n_norm = gc_ref.shape[1]

    def rope_qk(blk):
        return _rope(blk, cos_ref[...], sa_ref[...], sb_ref[...], half_qk)

    def rope_mla(blk):
        return _rope(blk, cosd_ref[...], sad_ref[...], sbd_ref[...], half_mla)

    def head_norm(blk, col):
        if col >= n_norm:
            return blk
        return rope_qk(blk * lax.rsqrt(_group_mean_sq(blk) + EPS) * gc_ref[:, col:col + LANES])

    _proj_chunks(h, w1_ref, qka_ref, lambda blk, col: rope_qk(blk))
    _proj_chunks(h, w2_ref, misc_ref, lambda blk, col: blk)
    _proj_chunks(h, w3_ref, qkc_ref, head_norm)
    _proj_chunks(h, w4_ref, gates_ref, lambda blk, col: jax.nn.sigmoid(blk))

    low = jnp.dot(h, w5_ref[...], preferred_element_type=F32)
    qn = _rms(low[:, :MLA_Q_LORA], gq_ref[...]).astype(BF16)
    kvn = _rms(low[:, MLA_Q_LORA:MLA_Q_LORA + MLA_KV_LORA], gkv_ref[...]).astype(BF16)
    k_rope = rope_mla(low[:, MLA_Q_LORA + MLA_KV_LORA:])
    _proj_chunks(qn, wuq_ref, qd_ref, lambda blk, col: rope_mla(blk) * mla_scale)
    _proj_chunks(kvn, wuk_ref, kd_ref, lambda blk, col: blk + k_rope)
    _proj_chunks(kvn, wuv_ref, vd_ref, lambda blk, col: blk)


def _in_proj(rows, h, weights, rope_qk, rope_mla, gain_c, g_q, g_kv, tm):
    n, k = h.shape
    n_lat_tiles = rows.n_lat // tm
    per = rows.S // tm
    row = lambda i: (i, 0)
    const = lambda i: (0, 0)
    table = pl.BlockSpec((tm, LANES), lambda i: (jnp.where(i < n_lat_tiles, i % per, per), 0))
    resident = lambda a: pl.BlockSpec(a.shape, const, pipeline_mode=pl.Buffered(1))
    w1, w2, w3, w4, w5, wuq, wuk, wuv = weights
    out_widths = [w1.shape[1], w2.shape[1], w3.shape[1], w4.shape[1], wuq.shape[1], wuk.shape[1], wuv.shape[1]]
    vecs = [gain_c, g_q.reshape(1, -1).astype(F32), g_kv.reshape(1, -1).astype(F32)]
    return pl.pallas_call(
        functools.partial(_in_proj_kernel, half_qk=rope_qk[3], half_mla=rope_mla[3],
                          mla_scale=(MLA_NOPE + MLA_ROPE) ** -0.5 * LOG2E),
        grid=(n // tm,),
        in_specs=([pl.BlockSpec((tm, k), row)] + [resident(w) for w in weights] + [table] * 6
                  + [pl.BlockSpec(v.shape, const) for v in vecs]),
        out_specs=[pl.BlockSpec((tm, w), row) for w in out_widths],
        out_shape=[jax.ShapeDtypeStruct((n, w), BF16) for w in out_widths],
        compiler_params=_params(("parallel",)),
        name="in_proj",
    )(h, *weights, *rope_qk[:3], *rope_mla[:3], *vecs)


def _lo_rows(n):
    return lax.broadcasted_iota(jnp.int32, (LANES, n), 0) < HALF


def _stack_heads(q):
    half = q.shape[1] // 2
    lo = lax.broadcasted_iota(jnp.int32, q.shape, 1) < half
    zero = jnp.zeros_like(q)
    return jnp.concatenate([jnp.where(lo, q, zero), jnp.where(lo, zero, q)], axis=0)


def _transpose_bf16(v):
    return v.astype(F32).T.astype(BF16)


def _attend_t(problems):
    all_scores = []
    for w, keys, _, biases in problems:
        scores = []
        for idx, k in enumerate(keys):
            s = lax.dot_general(k, w, NT_DIMS, preferred_element_type=F32)
            if biases is not None and biases[idx] is not None:
                s = s + biases[idx]
            scores.append(s)
        all_scores.append(scores)
    outs = []
    for scores, (_, _, vals_t, _) in zip(all_scores, problems):
        m = scores[0].max(axis=0, keepdims=True)
        for s in scores[1:]:
            m = jnp.maximum(m, s.max(axis=0, keepdims=True))
        denom = None
        out = None
        for s, vt in zip(scores, vals_t):
            e = jnp.exp2(s - m)
            d = e.sum(axis=0, keepdims=True)
            o = jnp.dot(vt, e.astype(BF16), preferred_element_type=F32)
            denom = d if denom is None else denom + d
            out = o if out is None else out + o
        outs.append(out / denom)
    return outs


def _attn_kernel(q_ref, kc_ref, kl_ref, vc_ref, vl_ref, *rest, mode, nq, want_ctx, lam_init, n_sub, qk_lanes,
                 shared_kv):
    vtc, vtl = rest[-2:]
    o_ref = rest[-3]
    extra = rest[:-3]
    t = pl.program_id(2)
    tq = q_ref.shape[0]

    @pl.when(t == 0)
    def _():
        vtc[...] = _transpose_bf16(vc_ref[...])
        vtl[...] = _transpose_bf16(vl_ref[...])

    def finish(u, o_t):
        o0_t, o1_t = o_t[:, :tq], o_t[:, tq:]
        cols = slice(u * LANES, (u + 1) * LANES)
        if mode == "diff":
            lq1, lk1, lq2, lk2, g_ref = extra
            lam = (jnp.exp(jnp.sum(lq1[...] * lk1[...], axis=-1, keepdims=True))
                   - jnp.exp(jnp.sum(lq2[...] * lk2[...], axis=-1, keepdims=True)) + lam_init)
            o = (o0_t - lam * o1_t).T
            o_ref[:, cols] = (_rms(o, g_ref[...]) * (1.0 - lam_init)).astype(BF16)
        else:
            o_ref[:, cols] = jnp.where(_lo_rows(tq), o0_t, o1_t).T.astype(BF16)

    def run(with_latent):
        problems = []
        for u in range(n_sub):
            ql = slice(u * qk_lanes, (u + 1) * qk_lanes)
            kl = slice(0, qk_lanes) if shared_kv else ql
            vr = slice(0, LANES) if shared_kv else slice(u * LANES, (u + 1) * LANES)
            keys, vals = [kc_ref[:, kl]], [vtc[vr, :]]
            if with_latent:
                keys.append(kl_ref[:, kl])
                vals.append(vtl[vr, :])
            problems.append((_stack_heads(q_ref[:, ql]), keys, vals, None))
        for u, o_t in enumerate(_attend_t(problems)):
            finish(u, o_t)

    latent_queries = functools.partial(run, True)
    context_queries = functools.partial(run, False)

    if want_ctx:
        pl.when(t < nq)(latent_queries)
        pl.when(t >= nq)(context_queries)
    else:
        latent_queries()


def _q_row_map(rows, tq, want_ctx):
    nq = rows.S // tq
    nc = rows.C // tq
    B = rows.B

    def qrow(b, t):
        if not want_ctx:
            return b * nq + t
        return jnp.where(t < nq, b * nq + t, B * nq + b * nc + (t - nq))

    return qrow, nq, nc


def _attention(rows, tq, want_ctx, mode, q_src, k_src, v_src, n_groups, qk_lanes=LANES, shared_kv=False,
               extra=(), lam_init=0.0, name="attn"):
    B, S, C = rows.B, rows.S, rows.C
    n_sub = PAIRS_PER_STEP
    qrow, nq, nc = _q_row_map(rows, tq, want_ctx)
    n_q_tiles = nq + (nc if want_ctx else 0)
    ctx_blk0 = rows.n_lat // C
    (qarr, q0), (karr, k0), (varr, v0) = q_src, k_src, v_src
    qw = n_sub * qk_lanes
    kw = qk_lanes if shared_kv else qw
    vw = LANES if shared_kv else n_sub * LANES
    assert n_groups % n_sub == 0 and (q0 * qk_lanes) % qw == 0 and (k0 * qk_lanes) % kw == 0 and (v0 * LANES) % vw == 0
    qb, kb, vb = q0 * qk_lanes // qw, k0 * qk_lanes // kw, v0 * LANES // vw
    kstep = 0 if shared_kv else 1
    in_specs = [pl.BlockSpec((tq, qw), lambda b, g, t: (qrow(b, t), qb + g)),
                pl.BlockSpec((C, kw), lambda b, g, t: (ctx_blk0 + b, kb + kstep * g)),
                pl.BlockSpec((S, kw), lambda b, g, t: (b, kb + kstep * g)),
                pl.BlockSpec((C, vw), lambda b, g, t: (ctx_blk0 + b, vb + kstep * g)),
                pl.BlockSpec((S, vw), lambda b, g, t: (b, vb + kstep * g))]
    args = [qarr, karr, karr, varr, varr]
    for e in extra:
        in_specs.append(pl.BlockSpec(e.shape, lambda b, g, t: (0, 0)))
        args.append(e)
    n_out_rows = rows.n_all if want_ctx else rows.n_lat
    return pl.pallas_call(
        functools.partial(_attn_kernel, mode=mode, nq=nq, want_ctx=want_ctx, lam_init=lam_init, n_sub=n_sub,
                          qk_lanes=qk_lanes, shared_kv=shared_kv),
        grid=(B, n_groups // n_sub, n_q_tiles),
        in_specs=in_specs,
        out_specs=pl.BlockSpec((tq, n_sub * LANES), lambda b, g, t: (qrow(b, t), g)),
        out_shape=jax.ShapeDtypeStruct((n_out_rows, n_groups * LANES), BF16),
        scratch_shapes=[pltpu.VMEM((vw, C), BF16), pltpu.VMEM((vw, S), BF16)],
        compiler_params=_params(("parallel", "parallel", "arbitrary")),
        name=name,
    )(*args)


NBR_Q_ROWS = 4
NBR_WIN_ROWS = NBR_Q_ROWS + NA_KH


def _nbr_kernel(q_ref, kc_ref, kl_ref, vc_ref, vl_ref, bias_ref, o_ref, vtc, *, n_row_blocks, grid_rows, want_ctx):
    t = pl.program_id(2)
    tq = q_ref.shape[0]
    n_win = NBR_WIN_ROWS * GRID_W

    @pl.when(t == 0)
    def _():
        vtc[...] = _transpose_bf16(vc_ref[...])

    def finish(o_t):
        o_ref[...] = jnp.where(_lo_rows(tq), o_t[:, :tq], o_t[:, tq:]).T.astype(BF16)

    def latent_queries():
        base = jnp.clip(t * NBR_Q_ROWS - NA_KH // 2, 0, grid_rows - NBR_WIN_ROWS)
        off = pl.multiple_of(base * GRID_W, GRID_W)
        kw = kl_ref[pl.ds(off, n_win), :]
        vtw = _transpose_bf16(vl_ref[pl.ds(off, n_win), :])
        problem = (_stack_heads(q_ref[...]), [kc_ref[...], kw], [vtc[...], vtw], [None, bias_ref[0, 0]])
        finish(_attend_t([problem])[0])

    def context_queries():
        finish(_attend_t([(_stack_heads(q_ref[...]), [kc_ref[...]], [vtc[...]], None)])[0])

    if want_ctx:
        pl.when(t < n_row_blocks)(latent_queries)
        pl.when(t >= n_row_blocks)(context_queries)
    else:
        latent_queries()


def _nbr_bias(rpb):
    kw = NA_KW
    n_h = rpb.shape[0]
    col = jnp.arange(GRID_W)
    col_start = jnp.clip(col - kw // 2, 0, GRID_W - kw)
    in_win = (col[None, :] >= col_start[:, None]) & (col[None, :] < col_start[:, None] + kw)
    dc = jnp.clip(col[None, :] - col[:, None], -(kw - 1), kw - 1) + (NA_KW - 1)
    onehot = (dc[:, :, None] == jnp.arange(2 * NA_KW - 1)[None, None, :]).astype(F32)
    by_col = jnp.einsum("hrc,qkc->hrqk", rpb.astype(F32) * LOG2E, onehot, precision=HIGHEST)
    g = jnp.arange(NBR_Q_ROWS)[:, None]
    j = jnp.arange(NBR_WIN_ROWS)[None, :]
    first_row = jnp.stack([jnp.zeros_like(g), g, jnp.full_like(g, NBR_Q_ROWS)])
    rel_off = jnp.array([NA_KH - 1, NA_KH - 1 - NBR_Q_ROWS, NA_KH - 1 - 2 * NBR_Q_ROWS])[:, None, None]
    dr = j[None] - g[None] + rel_off
    row_ok = (j[None] >= first_row) & (j[None] < first_row + NA_KH)
    picked = jnp.take(by_col, jnp.clip(dr, 0, 2 * NA_KH - 2).reshape(-1), axis=1)
    picked = picked.reshape(n_h, 3, NBR_Q_ROWS, NBR_WIN_ROWS, GRID_W, GRID_W)
    ok = row_ok[None, :, :, :, None, None] & in_win[None, None, None, None, :, :]
    bias = jnp.where(ok, picked, MASKED).reshape(n_h // 2, 2, 3, NBR_Q_ROWS, NBR_WIN_ROWS, GRID_W, GRID_W)
    bias = bias.transpose(2, 0, 4, 6, 1, 3, 5)
    return bias.reshape(3, n_h // 2, NBR_WIN_ROWS * GRID_W, 2 * NBR_Q_ROWS * GRID_W)


def _nbr_attention(rows, want_ctx, misc, bias, q_blk0, k_blk0, v_blk0):
    B, S, C = rows.B, rows.S, rows.C
    tq = NBR_Q_ROWS * GRID_W
    grid_rows = S // GRID_W
    assert grid_rows >= NBR_WIN_ROWS and grid_rows % NBR_Q_ROWS == 0 and C % tq == 0
    qrow, nq, nc = _q_row_map(rows, tq, want_ctx)
    n_q_tiles = nq + (nc if want_ctx else 0)
    ctx_blk0 = rows.n_lat // C
    n_groups = B_HEADS // 2
    n_out_rows = rows.n_all if want_ctx else rows.n_lat
    n_keys = NBR_WIN_ROWS * GRID_W

    def bias_map(g, b, t):
        return (jnp.where(t == 0, 0, jnp.where(t < nq - 1, 1, 2)), g, 0, 0)

    return pl.pallas_call(
        functools.partial(_nbr_kernel, n_row_blocks=nq, grid_rows=grid_rows, want_ctx=want_ctx),
        grid=(n_groups, B, n_q_tiles),
        in_specs=[pl.BlockSpec((tq, LANES), lambda g, b, t: (qrow(b, t), q_blk0 + g)),
                  pl.BlockSpec((C, LANES), lambda g, b, t: (ctx_blk0 + b, k_blk0 + g)),
                  pl.BlockSpec((S, LANES), lambda g, b, t: (b, k_blk0 + g)),
                  pl.BlockSpec((C, LANES), lambda g, b, t: (ctx_blk0 + b, v_blk0 + g)),
                  pl.BlockSpec((S, LANES), lambda g, b, t: (b, v_blk0 + g)),
                  pl.BlockSpec((1, 1, n_keys, 2 * tq), bias_map)],
        out_specs=pl.BlockSpec((tq, LANES), lambda g, b, t: (qrow(b, t), g)),
        out_shape=jax.ShapeDtypeStruct((n_out_rows, n_groups * LANES), BF16),
        scratch_shapes=[pltpu.VMEM((LANES, C), BF16)],
        compiler_params=_params(("parallel", "parallel", "arbitrary")),
        name="attn_nbr",
    )(misc, misc, misc, misc, misc, bias)


def _residual_epilogue(x, y, gate, g_post, nxt):
    x_new = x + gate * _rms(y, g_post)
    h = None
    if nxt is not None:
        g_pre, sc, sh = nxt
        h = _modulated_norm(x_new, g_pre, sc, sh)
    return x_new, h


def _dot_split(a, b):
    a_hi, b_hi = a.astype(BF16), b.astype(BF16)
    a_lo = (a - a_hi.astype(F32)).astype(BF16)
    b_lo = (b - b_hi.astype(F32)).astype(BF16)
    dot = functools.partial(jnp.dot, preferred_element_type=F32)
    return dot(a_hi, b_hi) + (dot(a_hi, b_lo) + dot(a_lo, b_hi))


def _route(h, w_router):
    logits = _dot_split(h, w_router)
    n_e = logits.shape[1]
    idx = lax.broadcasted_iota(jnp.int32, logits.shape, 1)
    top1 = logits.max(axis=-1, keepdims=True)
    i1 = jnp.where(logits == top1, idx, n_e).min(axis=-1, keepdims=True)
    rest = jnp.where(idx == i1, -jnp.inf, logits)
    top2 = rest.max(axis=-1, keepdims=True)
    i2 = jnp.where(rest == top2, idx, n_e).min(axis=-1, keepdims=True)
    e2 = jnp.exp(top2 - top1)
    w1 = 1.0 / (1.0 + e2)
    w2 = e2 / (1.0 + e2)
    k = lax.broadcasted_iota(jnp.int32, (logits.shape[0], 2), 1)
    return jnp.where(k == 0, i1, i2), jnp.where(k == 0, w1, w2)


def _merge_kernel(gates_ref, oa_ref, ob_ref, oc_ref, od_ref, wbr_ref, wout_ref, x_ref, gate_ref, gpost_ref,
                  gpre_ref, sc_ref, sh_ref, *rest, routed):
    m = None
    for j, o_ref in enumerate((oa_ref, ob_ref, oc_ref, od_ref)):
        br = jnp.dot(o_ref[...], wbr_ref[j], preferred_element_type=F32)
        term = gates_ref[:, j * D_MODEL:(j + 1) * D_MODEL].astype(F32) * br
        m = term if m is None else m + term
    y = jnp.dot(m.astype(BF16), wout_ref[...], preferred_element_type=F32)
    x_new, h = _residual_epilogue(x_ref[...], y, gate_ref[0], gpost_ref[...],
                                  (gpre_ref[...], sc_ref[0], sh_ref[0]))
    if routed:
        wr_ref, x_out, h_out, idx_out, wts_out = rest
        idx_out[...], wts_out[...] = _route(h, wr_ref[...])
    else:
        x_out, h_out = rest
    x_out[...] = x_new
    h_out[...] = h.astype(h_out.dtype)


def _merge(rows, n_rows, gates, o_all, wbr, wout, x, mods, layer, g_post, g_pre, w_router, tm):
    row = lambda i: (i, 0)
    const2 = lambda i: (0, 0)
    routed = w_router is not None
    in_specs = [pl.BlockSpec((tm, N_BRANCH * D_MODEL), row)]
    in_specs += [pl.BlockSpec((tm, o.shape[1]), row) for o in o_all]
    in_specs += [pl.BlockSpec(wbr.shape, lambda i: (0, 0, 0)), pl.BlockSpec(wout.shape, const2),
                 pl.BlockSpec((tm, D_MODEL), row), rows.mod_spec(layer, 2, tm),
                 pl.BlockSpec((1, D_MODEL), const2), pl.BlockSpec((1, D_MODEL), const2),
                 rows.mod_spec(layer, 4, tm), rows.mod_spec(layer, 3, tm)]
    args = [gates, *o_all, wbr, wout, x, mods, g_post.reshape(1, -1), g_pre.reshape(1, -1), mods, mods]
    out_specs = [pl.BlockSpec((tm, D_MODEL), row), pl.BlockSpec((tm, D_MODEL), row)]
    out_shape = [jax.ShapeDtypeStruct((n_rows, D_MODEL), F32),
                 jax.ShapeDtypeStruct((n_rows, D_MODEL), F32 if routed else BF16)]
    if routed:
        in_specs.append(pl.BlockSpec(w_router.shape, const2))
        args.append(w_router)
        out_specs += [pl.BlockSpec((tm, 2), row), pl.BlockSpec((tm, 2), row)]
        out_shape += [jax.ShapeDtypeStruct((n_rows, 2), jnp.int32), jax.ShapeDtypeStruct((n_rows, 2), F32)]
    return pl.pallas_call(
        functools.partial(_merge_kernel, routed=routed),
        grid=(n_rows // tm,),
        in_specs=in_specs, out_specs=out_specs, out_shape=out_shape,
        compiler_params=_params(("parallel",)),
        name="merge_out",
    )(*args)


FFN_CHUNK = 256


def _ffn_kernel(h_ref, w1_ref, w3_ref, w2_ref, x_ref, gate_ref, gpost_ref, *rest, has_next):
    if has_next:
        gpre_ref, sc_ref, sh_ref, x_out, h_out = rest
    else:
        (x_out,) = rest
    h = h_ref[...]
    y = None
    for c0 in range(0, w1_ref.shape[1], FFN_CHUNK):
        cs = slice(c0, c0 + FFN_CHUNK)
        a = jnp.dot(h, w1_ref[:, cs], preferred_element_type=F32)
        b = jnp.dot(h, w3_ref[:, cs], preferred_element_type=F32)
        part = jnp.dot((a * jax.nn.sigmoid(a) * b).astype(BF16), w2_ref[cs, :], preferred_element_type=F32)
        y = part if y is None else y + part
    nxt = (gpre_ref[...], sc_ref[0], sh_ref[0]) if has_next else None
    x_new, hn = _residual_epilogue(x_ref[...], y, gate_ref[0], gpost_ref[...], nxt)
    x_out[...] = x_new
    if has_next:
        h_out[...] = hn.astype(BF16)


def _ffn(rows, n_rows, h, w1, w3, w2, x, mods, layer, g_post, g_pre_next, tm):
    row = lambda i: (i, 0)
    const2 = lambda i: (0, 0)
    has_next = g_pre_next is not None
    assert w1.shape[1] % FFN_CHUNK == 0
    resident = lambda a: pl.BlockSpec(a.shape, const2, pipeline_mode=pl.Buffered(1))
    in_specs = [pl.BlockSpec((tm, D_MODEL), row), resident(w1), resident(w3), resident(w2),
                pl.BlockSpec((tm, D_MODEL), row), rows.mod_spec(layer, 5, tm), pl.BlockSpec((1, D_MODEL), const2)]
    args = [h, w1, w3, w2, x, mods, g_post.reshape(1, -1)]
    out_specs = [pl.BlockSpec((tm, D_MODEL), row)]
    out_shape = [jax.ShapeDtypeStruct((n_rows, D_MODEL), F32)]
    if has_next:
        in_specs += [pl.BlockSpec((1, D_MODEL), const2), rows.mod_spec(layer + 1, 1, tm),
                     rows.mod_spec(layer + 1, 0, tm)]
        args += [g_pre_next.reshape(1, -1), mods, mods]
        out_specs.append(pl.BlockSpec((tm, D_MODEL), row))
        out_shape.append(jax.ShapeDtypeStruct((n_rows, D_MODEL), BF16))
    return pl.pallas_call(
        functools.partial(_ffn_kernel, has_next=has_next),
        grid=(n_rows // tm,),
        in_specs=in_specs, out_specs=out_specs, out_shape=out_shape,
        compiler_params=_params(("parallel",)),
        name="ffn_dense",
    )(*args)


def _expert_slots(idx, n_tok, tme):
    n_e = N_EXPERTS
    n_asg = 2 * n_tok
    e_flat = idx.reshape(-1)
    onehot = (e_flat[:, None] == jnp.arange(n_e, dtype=jnp.int32)[None, :]).astype(jnp.int32)
    csum = jnp.cumsum(onehot, axis=0)
    counts = csum[-1]
    group = ((counts + tme - 1) // tme) * tme
    group_end = jnp.cumsum(group)
    group_start = group_end - group
    slot = jnp.sum(onehot * (csum - 1 + group_start[None, :]), axis=1)
    n_tiles = n_asg // tme + n_e
    token_of_slot = jnp.zeros((n_tiles * tme,), jnp.int32).at[slot].set(
        jnp.arange(n_asg, dtype=jnp.int32) // 2, unique_indices=True)
    tile_start = jnp.arange(n_tiles, dtype=jnp.int32) * tme
    tile_e = jnp.minimum(jnp.sum((tile_start[:, None] >= group_end[None, :]).astype(jnp.int32), axis=1), n_e - 1)
    sel = (tile_e[:, None] == jnp.arange(n_e, dtype=jnp.int32)[None, :]).astype(jnp.int32)
    filled_end = jnp.sum(sel * (group_start + counts)[None, :], axis=1)
    tile_rows = jnp.clip(filled_end - tile_start, 0, tme)
    return slot.astype(jnp.int32), token_of_slot, tile_e.astype(jnp.int32), tile_rows.astype(jnp.int32)


GATHER_BLOCK = 256
GATHER_UNROLL = 8


def _row_gather(src_hbm, dst, sem, index_of_row, n_blocks):
    def block(blk, carry):
        def body(i, c):
            r = blk * GATHER_BLOCK + i
            src_row = 0 if index_of_row is None else index_of_row(r)
            cp = pltpu.make_async_copy(src_hbm.at[pl.ds(src_row, 1), :], dst.at[pl.ds(r, 1), :], sem)
            if index_of_row is None:
                cp.wait()
            else:
                cp.start()
            return c

        return lax.fori_loop(0, GATHER_BLOCK, body, carry, unroll=GATHER_UNROLL)

    lax.fori_loop(0, n_blocks, block, 0)


def _moe_ffn_kernel(tile_e, tile_rows, tok, h_hbm, w1_ref, w3_ref, w2_ref, y_ref, xg, xb, acc, sem, *, tme, sub):
    t = pl.program_id(0)
    f = pl.program_id(1)
    n_t = pl.num_programs(0)
    buf = t % 2

    def gather_blocks(tile):
        return (tile_rows[tile] + sub - 1) // sub

    def start(tile, b):
        _row_gather(h_hbm, xg.at[b], sem.at[b], lambda r: tok[tile * tme + r], gather_blocks(tile))

    @pl.when(f == 0)
    def _():
        @pl.when(t == 0)
        def _():
            start(0, 0)

        @pl.when(t + 1 < n_t)
        def _():
            start(t + 1, 1 - buf)

        _row_gather(h_hbm, xg.at[buf], sem.at[buf], None, gather_blocks(t))
        acc[...] = jnp.zeros_like(acc)

    def swiglu_rows(rs):
        @pl.when(f == 0)
        def _():
            xb[rs, :] = xg[buf, rs, :].astype(BF16)

        x = xb[rs, :]
        a = jnp.dot(x, w1_ref[0].astype(BF16), preferred_element_type=F32)
        b = jnp.dot(x, w3_ref[0].astype(BF16), preferred_element_type=F32)
        acc[rs, :] += jnp.dot((a * jax.nn.sigmoid(a) * b).astype(BF16), w2_ref[0].astype(BF16),
                              preferred_element_type=F32)

    full = tile_rows[t] == tme

    @pl.when(full)
    def _():
        swiglu_rows(slice(0, tme))

    for sb in range(tme // sub):
        @pl.when(jnp.logical_and(jnp.logical_not(full), sb * sub < tile_rows[t]))
        def _():
            swiglu_rows(slice(sb * sub, (sb + 1) * sub))

    @pl.when(f == pl.num_programs(1) - 1)
    def _():
        y_ref[...] = acc[...]


def _moe_ffn(h, token_of_slot, tile_e, tile_rows, w1, w3, w2, tme, tf):
    n_e, d, ff = w1.shape
    n_tiles = tile_e.shape[0]
    sub = GATHER_BLOCK
    grid_spec = pltpu.PrefetchScalarGridSpec(
        num_scalar_prefetch=3,
        grid=(n_tiles, ff // tf),
        in_specs=[pl.BlockSpec(memory_space=pl.ANY),
                  pl.BlockSpec((1, d, tf), lambda t, f, te, tr, tok: (te[t], 0, f)),
                  pl.BlockSpec((1, d, tf), lambda t, f, te, tr, tok: (te[t], 0, f)),
                  pl.BlockSpec((1, tf, d), lambda t, f, te, tr, tok: (te[t], f, 0))],
        out_specs=pl.BlockSpec((tme, d), lambda t, f, te, tr, tok: (t, 0)),
        scratch_shapes=[pltpu.VMEM((2, tme, d), F32), pltpu.VMEM((tme, d), BF16), pltpu.VMEM((tme, d), F32),
                        pltpu.SemaphoreType.DMA((2,))],
    )
    return pl.pallas_call(
        functools.partial(_moe_ffn_kernel, tme=tme, sub=sub),
        grid_spec=grid_spec,
        out_shape=jax.ShapeDtypeStruct((n_tiles * tme, d), F32),
        compiler_params=_params(("arbitrary", "arbitrary")),
        name="ffn_moe",
    )(tile_e, tile_rows, token_of_slot, h, w1, w3, w2)


def _moe_combine_kernel(slot, y_hbm, wts_ref, x_ref, gate_ref, gpost_ref, *rest, tm, has_next):
    if has_next:
        gpre_ref, sc_ref, sh_ref, x_out, h_out, yb, sem = rest
    else:
        x_out, yb, sem = rest
    i = pl.program_id(0)
    n_i = pl.num_programs(0)
    buf = i % 2

    def start(tile, b):
        for k in range(2):
            _row_gather(y_hbm, yb.at[b, k], sem.at[b], lambda r, k=k: slot[2 * (tile * tm + r) + k], tm // GATHER_BLOCK)

    @pl.when(i == 0)
    def _():
        start(0, 0)

    @pl.when(i + 1 < n_i)
    def _():
        start(i + 1, 1 - buf)

    for k in range(2):
        _row_gather(y_hbm, yb.at[buf, k], sem.at[buf], None, tm // GATHER_BLOCK)
    wts = wts_ref[...]
    y = wts[:, 0:1] * yb[buf, 0] + wts[:, 1:2] * yb[buf, 1]
    nxt = (gpre_ref[...], sc_ref[0], sh_ref[0]) if has_next else None
    x_new, hn = _residual_epilogue(x_ref[...], y, gate_ref[0], gpost_ref[...], nxt)
    x_out[...] = x_new
    if has_next:
        h_out[...] = hn.astype(BF16)


def _moe_combine(rows, n_rows, slot, y, wts, x, mods, layer, g_post, g_pre_next, tm):
    has_next = g_pre_next is not None
    row = lambda i, s: (i, 0)
    const2 = lambda i, s: (0, 0)

    mod_spec = lambda lyr, k: rows.mod_spec(lyr, k, tm)
    in_specs = [pl.BlockSpec(memory_space=pl.ANY), pl.BlockSpec((tm, 2), row),
                pl.BlockSpec((tm, D_MODEL), row), mod_spec(layer, 5), pl.BlockSpec((1, D_MODEL), const2)]
    args = [y, wts, x, mods, g_post.reshape(1, -1)]
    out_specs = [pl.BlockSpec((tm, D_MODEL), row)]
    out_shape = [jax.ShapeDtypeStruct((n_rows, D_MODEL), F32)]
    if has_next:
        in_specs += [pl.BlockSpec((1, D_MODEL), const2), mod_spec(layer + 1, 1), mod_spec(layer + 1, 0)]
        args += [g_pre_next.reshape(1, -1), mods, mods]
        out_specs.append(pl.BlockSpec((tm, D_MODEL), row))
        out_shape.append(jax.ShapeDtypeStruct((n_rows, D_MODEL), BF16))
    grid_spec = pltpu.PrefetchScalarGridSpec(
        num_scalar_prefetch=1, grid=(n_rows // tm,), in_specs=in_specs, out_specs=out_specs,
        scratch_shapes=[pltpu.VMEM((2, 2, tm, D_MODEL), F32), pltpu.SemaphoreType.DMA((2,))])
    return pl.pallas_call(
        functools.partial(_moe_combine_kernel, tm=tm, has_next=has_next),
        grid_spec=grid_spec, out_shape=out_shape,
        compiler_params=_params(("arbitrary",)),
        name="moe_combine",
    )(slot, *args)


def _in_proj_weights(w_in):
    widths = (512, 512, 512, 512, 512, 512, 512, 128, 128, MLA_Q_LORA, MLA_KV_LORA, MLA_ROPE, N_BRANCH * D_MODEL)
    cuts, o = [], 0
    for w in widths:
        cuts.append((o, o + w))
        o += w
    aq, ak, av, bq, bk, bv, cq, ck, cv, dqa, dkva, dkr, gates = [w_in[:, a:b] for a, b in cuts]
    cq = cq.reshape(-1, 2, 4, C_HEAD_DIM).transpose(0, 2, 1, 3).reshape(-1, C_HEADS * C_HEAD_DIM)
    dkr_pad = jnp.zeros((w_in.shape[0], LANES), w_in.dtype).at[:, MLA_NOPE:MLA_NOPE + MLA_ROPE].set(dkr)
    w1 = jnp.concatenate([aq * (A_QK_DIM ** -0.5 * LOG2E), ak], axis=1)
    w2 = jnp.concatenate([av, bq * (B_HEAD_DIM ** -0.5 * LOG2E), bk, bv], axis=1)
    w3 = jnp.concatenate([cq, ck, cv], axis=1)
    w5 = jnp.concatenate([dqa, dkva, dkr_pad], axis=1)
    return [w.astype(BF16) for w in (w1, w2, w3, gates, w5)]


def _mla_weights(w_uq, w_ukv):
    dq = MLA_NOPE + MLA_ROPE
    wuq = jnp.pad(w_uq.reshape(-1, D_HEADS, dq), ((0, 0), (0, 0), (0, LANES - dq))).reshape(-1, D_HEADS * LANES)
    kv = w_ukv.reshape(-1, D_HEADS, MLA_NOPE + MLA_V)
    wuk = jnp.pad(kv[:, :, :MLA_NOPE], ((0, 0), (0, 0), (0, LANES - MLA_NOPE))).reshape(-1, D_HEADS * LANES)
    wuv = kv[:, :, MLA_NOPE:].reshape(-1, D_HEADS * MLA_V)
    return [wuq.astype(BF16), wuk.astype(BF16), wuv.astype(BF16)]


def kernel(x, c, ctx, c_ctx, w_ada, b_ada, g_mix_pre, g_mix_post, g_ffn_pre, g_ffn_post, w_in, lam_q1, lam_k1, lam_q2, lam_k2, g_diff_sub, na_rpb, g_qnorm, g_knorm, g_q_lora, w_uq, g_kv_lora, w_ukv, w_br_a, w_br_b, w_br_c, w_br_d, w_out, w1_dense, w3_dense, w2_dense, w_router, w1_moe, w3_moe, w2_moe):
    B, S, D = x.shape
    C = ctx.shape[1]
    depth = w_in.shape[0]
    assert D == D_MODEL and B + 1 <= MOD_ROWS
    rows = _Rows(B, S, C)
    tq = 256
    tm_proj = _pick_tile((1024, 512, 256), S, B * C)
    tm_row = _pick_tile((512, 256), S, B * C)

    cvec = jnp.zeros((MOD_ROWS, D), F32).at[:B].set(c).at[B].set(c_ctx)
    mods = _mods(cvec, w_ada, b_ada)

    rope_qk = _rope_tables(S, A_QK_DIM, 0, tm_proj)
    rope_mla = _rope_tables(S, MLA_ROPE, MLA_NOPE, tm_proj)

    xs = jnp.concatenate([x.reshape(B * S, D), ctx.reshape(B * C, D)], axis=0)
    h = _prenorm(rows, xs, g_mix_pre[0], mods, 0, tm_proj)

    for i in range(depth):
        last = i == depth - 1
        want_ctx = not last
        n_rows = rows.n_all if want_ctx else rows.n_lat
        lam_init = 0.8 - 0.6 * math.exp(-0.3 * i)

        gain_c = jnp.concatenate([jnp.tile(g_qnorm[i], C_HEADS) * (C_HEAD_DIM ** -0.5 * LOG2E),
                                  jnp.tile(g_knorm[i], C_KV_HEADS)]).reshape(1, -1).astype(F32)
        proj_weights = _in_proj_weights(w_in[i]) + _mla_weights(w_uq[i], w_ukv[i])
        qka, misc, qkc, gates, qd, kd, vd = _in_proj(rows, h, proj_weights, rope_qk, rope_mla, gain_c,
                                                     g_q_lora[i], g_kv_lora[i], tm_row)

        lam_vecs = [v[i].reshape(1, -1).astype(F32) for v in (lam_q1, lam_k1, lam_q2, lam_k2)]
        oa = _attention(rows, tq, want_ctx, "diff",
                        q_src=(qka, 0), k_src=(qka, A_HEADS), v_src=(misc, 0), n_groups=A_HEADS,
                        extra=lam_vecs + [g_diff_sub[i].reshape(1, -1).astype(F32)], lam_init=lam_init,
                        name="attn_diff")
        ob = _nbr_attention(rows, want_ctx, misc, _nbr_bias(na_rpb[i]), q_blk0=4, k_blk0=8, v_blk0=12)
        oc = _attention(rows, tq, want_ctx, "select",
                        q_src=(qkc, 0), k_src=(qkc, 4), v_src=(qkc, 5), n_groups=C_HEADS // 2, shared_kv=True,
                        name="attn_gqa")
        od = _attention(rows, tq, want_ctx, "select",
                        q_src=(qd, 0), k_src=(kd, 0), v_src=(vd, 0),
                        n_groups=D_HEADS // 2, qk_lanes=2 * LANES, name="attn_mla")

        wbc = w_br_c[i].reshape(2, 4, C_HEAD_DIM, D).transpose(1, 0, 2, 3).reshape(C_HEADS * C_HEAD_DIM, D)
        wbr = jnp.stack([w_br_a[i], w_br_b[i], wbc, w_br_d[i]]).astype(BF16)
        moe = i % 2 == 1
        j = i // 2
        merged = _merge(rows, n_rows, gates, (oa, ob, oc, od), wbr, w_out[i].astype(BF16), xs, mods, i,
                        g_mix_post[i], g_ffn_pre[i], w_router[j] if moe else None, tm_row)
        g_pre_next = None if last else g_mix_pre[i + 1]
        if moe:
            xs, h2, route_idx, route_wts = merged
            slot, token_of_slot, tile_e, tile_rows = _expert_slots(route_idx, n_rows, MOE_TILE)
            y = _moe_ffn(h2, token_of_slot, tile_e, tile_rows, w1_moe[j], w3_moe[j], w2_moe[j], MOE_TILE, 512)
            outs = _moe_combine(rows, n_rows, slot, y, route_wts, xs, mods, i, g_ffn_post[i], g_pre_next, 256)
        else:
            xs, h2 = merged
            outs = _ffn(rows, n_rows, h2, w1_dense[j].astype(BF16), w3_dense[j].astype(BF16),
                        w2_dense[j].astype(BF16), xs, mods, i, g_ffn_post[i], g_pre_next, tm_row)
        if last:
            xs = outs[0]
        else:
            xs, h = outs
    return xs[:B * S].reshape(B, S, D)
```

```python
import functools
import math

import jax
import jax.numpy as jnp
from jax import lax
from jax.experimental import pallas as pl
from jax.experimental.pallas import tpu as pltpu

F32 = jnp.float32
BF16 = jnp.bfloat16
HIGHEST = lax.Precision.HIGHEST

D_MODEL = 1024
GRID_W = 64
ROPE_BASE = 10000.0
EPS = 1e-6
N_BRANCH = 4
A_HEADS = 4
A_QK_DIM = 64
A_V_DIM = 128
B_HEADS = 8
B_HEAD_DIM = 64
NA_KH = 8
NA_KW = 16
C_HEADS = 8
C_KV_HEADS = 2
C_HEAD_DIM = 64
D_HEADS = 8
MLA_Q_LORA = 256
MLA_KV_LORA = 128
MLA_NOPE = 64
MLA_ROPE = 32
MLA_V = 64
N_EXPERTS = 8

LANES = 128
HALF = LANES // 2
MASKED = -1e30
LOG2E = math.log2(math.e)
PAIRS_PER_STEP = 4
N_MOD = 6
MOD_ROWS = 16
MOE_TILE = 1024
VMEM_LIMIT = 56 * 1024 * 1024

NT_DIMS = (((1,), (1,)), ((), ()))


def _pick_tile(candidates, *dims):
    for t in candidates:
        if all(d % t == 0 for d in dims):
            return t
    raise ValueError(f"no tile in {candidates} divides {dims}")


def _params(sem):
    return pltpu.CompilerParams(dimension_semantics=sem, vmem_limit_bytes=VMEM_LIMIT)


def _rms(x, g):
    return x * lax.rsqrt(jnp.mean(x * x, axis=-1, keepdims=True) + EPS) * g


def _modulated_norm(x, g_pre, sc, sh):
    return _rms(x, g_pre) * (1.0 + sc) + sh


def _rope(x, cos, sin_a, sin_b, half):
    up = pltpu.roll(x, LANES - half, axis=1)
    dn = pltpu.roll(x, half, axis=1)
    return x * cos + up * sin_a + dn * sin_b


def _group_mean_sq(x):
    r = lax.broadcasted_iota(jnp.int32, (LANES, LANES), 0) // HALF
    c = lax.broadcasted_iota(jnp.int32, (LANES, LANES), 1) // HALF
    ones = jnp.where(r == c, 1.0, 0.0).astype(BF16)
    sq = x * x
    hi = sq.astype(BF16)
    lo = (sq - hi.astype(F32)).astype(BF16)
    ss = jnp.dot(hi, ones, preferred_element_type=F32) + jnp.dot(lo, ones, preferred_element_type=F32)
    return ss * (1.0 / HALF)


def _rope_tables(seq_len, d_rot, lane_off, pad_rows):
    t = jnp.arange(seq_len)
    row = (t // GRID_W).astype(F32)
    col = (t % GRID_W).astype(F32)
    d_ax = d_rot // 2
    half = d_ax // 2
    inv = ROPE_BASE ** (-jnp.arange(0, d_ax, 2, dtype=F32) / d_ax)
    ang_r = row[:, None] * inv[None]
    ang_c = col[:, None] * inv[None]
    ang = jnp.concatenate([ang_r, ang_r, ang_c, ang_c], axis=-1)
    cos, sin = jnp.cos(ang), jnp.sin(ang)
    first = (jnp.arange(d_rot) % d_ax) < half
    sin_a = jnp.where(first[None], -sin, 0.0)
    sin_b = jnp.where(first[None], 0.0, sin)
    reps = (LANES - lane_off) // d_rot if lane_off == 0 else 1

    def place(tab, fill):
        tab = jnp.tile(tab, (1, reps))
        full = jnp.full((seq_len, LANES), fill, F32)
        full = full.at[:, lane_off:lane_off + tab.shape[1]].set(tab)
        return jnp.concatenate([full, jnp.full((pad_rows, LANES), fill, F32)], axis=0)

    return place(cos, 1.0), place(sin_a, 0.0), place(sin_b, 0.0), half


def _mods_kernel(c_ref, w_ref, b_ref, o_ref):
    c = c_ref[...]
    s = c * jax.nn.sigmoid(c)
    o_ref[0] = lax.dot_general(s, w_ref[0], (((1,), (0,)), ((), ())), precision=HIGHEST,
                               preferred_element_type=F32) + b_ref[0]


def _mods(cvec, w_ada, b_ada):
    n_layers, d, n6 = w_ada.shape
    tn = n6 // 4
    out = pl.pallas_call(
        _mods_kernel,
        grid=(n_layers, n6 // tn),
        in_specs=[pl.BlockSpec((MOD_ROWS, d), lambda l, j: (0, 0)),
                  pl.BlockSpec((1, d, tn), lambda l, j: (l, 0, j)),
                  pl.BlockSpec((1, 1, tn), lambda l, j: (l, 0, j))],
        out_specs=pl.BlockSpec((1, MOD_ROWS, tn), lambda l, j: (l, 0, j)),
        out_shape=jax.ShapeDtypeStruct((n_layers, MOD_ROWS, n6), F32),
        compiler_params=_params(("arbitrary", "arbitrary")),
        name="adaln_mods",
    )(cvec, w_ada, b_ada.reshape(n_layers, 1, n6))
    return out.reshape(n_layers * MOD_ROWS * N_MOD, 1, d)


class _Rows:
    def __init__(self, batch, seq, ctx_len):
        self.B, self.S, self.C = batch, seq, ctx_len
        self.n_lat = batch * seq
        self.n_all = batch * (seq + ctx_len)

    def mod_spec(self, layer, k, tm):
        n_lat_tiles = self.n_lat // tm
        per = self.S // tm
        B = self.B

        def imap(i, *_):
            brow = jnp.where(i < n_lat_tiles, i // per, B)
            return ((layer * MOD_ROWS + brow) * N_MOD + k, 0, 0)

        return pl.BlockSpec((1, 1, D_MODEL), imap)


def _prenorm_kernel(x_ref, g_ref, sc_ref, sh_ref, h_ref):
    h_ref[...] = _modulated_norm(x_ref[...], g_ref[...], sc_ref[0], sh_ref[0]).astype(BF16)


def _prenorm(rows, x, g_pre, mods, layer, tm):
    n = x.shape[0]
    row = lambda i: (i, 0)
    return pl.pallas_call(
        _prenorm_kernel,
        grid=(n // tm,),
        in_specs=[pl.BlockSpec((tm, D_MODEL), row),
                  pl.BlockSpec((1, D_MODEL), lambda i: (0, 0)),
                  rows.mod_spec(layer, 1, tm), rows.mod_spec(layer, 0, tm)],
        out_specs=pl.BlockSpec((tm, D_MODEL), row),
        out_shape=jax.ShapeDtypeStruct((n, D_MODEL), BF16),
        compiler_params=_params(("parallel",)),
        name="prenorm",
    )(x, g_pre.reshape(1, -1), mods, mods)


PROJ_CHUNK = 512


def _proj_chunks(x, w_ref, o_ref, epilogue):
    n = w_ref.shape[1]
    for c0 in range(0, n, PROJ_CHUNK):
        c1 = min(c0 + PROJ_CHUNK, n)
        acc = jnp.dot(x, w_ref[:, c0:c1], preferred_element_type=F32)
        for s in range((c1 - c0) // LANES):
            col = c0 + s * LANES
            o_ref[:, col:col + LANES] = epilogue(acc[:, s * LANES:(s + 1) * LANES], col).astype(BF16)


def _in_proj_kernel(h_ref, w1_ref, w2_ref, w3_ref, w4_ref, w5_ref, wuq_ref, wuk_ref, wuv_ref,
                    cos_ref, sa_ref, sb_ref, cosd_ref, sad_ref, sbd_ref, gc_ref, gq_ref, gkv_ref,
                    qka_ref, misc_ref, qkc_ref, gates_ref, qd_ref, kd_ref, vd_ref, *, half_qk, half_mla, mla_scale):
    h = h_ref[...]
    n_norm = gc_ref.shape[1]

    def rope_qk(blk):
        return _rope(blk, cos_ref[...], sa_ref[...], sb_ref[...], half_qk)

    def rope_mla(blk):
        return _rope(blk, cosd_ref[...], sad_ref[...], sbd_ref[...], half_mla)

    def head_norm(blk, col):
        if col >= n_norm:
            return blk
        return rope_qk(blk * lax.rsqrt(_group_mean_sq(blk) + EPS) * gc_ref[:, col:col + LANES])

    _proj_chunks(h, w1_ref, qka_ref, lambda blk, col: rope_qk(blk))
    _proj_chunks(h, w2_ref, misc_ref, lambda blk, col: blk)
    _proj_chunks(h, w3_ref, qkc_ref, head_norm)
    _proj_chunks(h, w4_ref, gates_ref, lambda blk, col: jax.nn.sigmoid(blk))

    low = jnp.dot(h, w5_ref[...], preferred_element_type=F32)
    qn = _rms(low[:, :MLA_Q_LORA], gq_ref[...]).astype(BF16)
    kvn = _rms(low[:, MLA_Q_LORA:MLA_Q_LORA + MLA_KV_LORA], gkv_ref[...]).astype(BF16)
    k_rope = rope_mla(low[:, MLA_Q_LORA + MLA_KV_LORA:])
    _proj_chunks(qn, wuq_ref, qd_ref, lambda blk, col: rope_mla(blk) * mla_scale)
    _proj_chunks(kvn, wuk_ref, kd_ref, lambda blk, col: blk + k_rope)
    _proj_chunks(kvn, wuv_ref, vd_ref, lambda blk, col: blk)


def _in_proj(rows, h, weights, rope_qk, rope_mla, gain_c, g_q, g_kv, tm):
    n, k = h.shape
    n_lat_tiles = rows.n_lat // tm
    per = rows.S // tm
    row = lambda i: (i, 0)
    const = lambda i: (0, 0)
    table = pl.BlockSpec((tm, LANES), lambda i: (jnp.where(i < n_lat_tiles, i % per, per), 0))
    resident = lambda a: pl.BlockSpec(a.shape, const, pipeline_mode=pl.Buffered(1))
    w1, w2, w3, w4, w5, wuq, wuk, wuv = weights
    out_widths = [w1.shape[1], w2.shape[1], w3.shape[1], w4.shape[1], wuq.shape[1], wuk.shape[1], wuv.shape[1]]
    vecs = [gain_c, g_q.reshape(1, -1).astype(F32), g_kv.reshape(1, -1).astype(F32)]
    return pl.pallas_call(
        functools.partial(_in_proj_kernel, half_qk=rope_qk[3], half_mla=rope_mla[3],
                          mla_scale=(MLA_NOPE + MLA_ROPE) ** -0.5 * LOG2E),
        grid=(n // tm,),
        in_specs=([pl.BlockSpec((tm, k), row)] + [resident(w) for w in weights] + [table] * 6
                  + [pl.BlockSpec(v.shape, const) for v in vecs]),
        out_specs=[pl.BlockSpec((tm, w), row) for w in out_widths],
        out_shape=[jax.ShapeDtypeStruct((n, w), BF16) for w in out_widths],
        compiler_params=_params(("parallel",)),
        name="in_proj",
    )(h, *weights, *rope_qk[:3], *rope_mla[:3], *vecs)


def _lo_rows(n):
    return lax.broadcasted_iota(jnp.int32, (LANES, n), 0) < HALF


def _stack_heads(q):
    half = q.shape[1] // 2
    lo = lax.broadcasted_iota(jnp.int32, q.shape, 1) < half
    zero = jnp.zeros_like(q)
    return jnp.concatenate([jnp.where(lo, q, zero), jnp.where(lo, zero, q)], axis=0)


def _transpose_bf16(v):
    return v.astype(F32).T.astype(BF16)


def _attend_t(problems):
    all_scores = []
    for w, keys, _, biases in problems:
        scores = []
        for idx, k in enumerate(keys):
            s = lax.dot_general(k, w, NT_DIMS, preferred_element_type=F32)
            if biases is not None and biases[idx] is not None:
                s = s + biases[idx].astype(F32)
            scores.append(s)
        all_scores.append(scores)
    outs = []
    for scores, (_, _, vals_t, _) in zip(all_scores, problems):
        m = scores[0].max(axis=0, keepdims=True)
        for s in scores[1:]:
            m = jnp.maximum(m, s.max(axis=0, keepdims=True))
        denom = None
        out = None
        for s, vt in zip(scores, vals_t):
            e = jnp.exp2(s - m)
            d = e.sum(axis=0, keepdims=True)
            o = jnp.dot(vt, e.astype(BF16), preferred_element_type=F32)
            denom = d if denom is None else denom + d
            out = o if out is None else out + o
        outs.append(out / denom)
    return outs


def _attn_kernel(q_ref, kc_ref, kl_ref, vc_ref, vl_ref, *rest, mode, nq, want_ctx, lam_init, n_sub, qk_lanes,
                 shared_kv):
    vtc, vtl = rest[-2:]
    o_ref = rest[-3]
    extra = rest[:-3]
    t = pl.program_id(2)
    tq = q_ref.shape[0]

    @pl.when(t == 0)
    def _():
        vtc[...] = _transpose_bf16(vc_ref[...])
        vtl[...] = _transpose_bf16(vl_ref[...])

    def finish(u, o_t):
        o0_t, o1_t = o_t[:, :tq], o_t[:, tq:]
        cols = slice(u * LANES, (u + 1) * LANES)
        if mode == "diff":
            lq1, lk1, lq2, lk2, g_ref = extra
            lam = (jnp.exp(jnp.sum(lq1[...] * lk1[...], axis=-1, keepdims=True))
                   - jnp.exp(jnp.sum(lq2[...] * lk2[...], axis=-1, keepdims=True)) + lam_init)
            o = (o0_t - lam * o1_t).T
            o_ref[:, cols] = (_rms(o, g_ref[...]) * (1.0 - lam_init)).astype(BF16)
        else:
            o_ref[:, cols] = jnp.where(_lo_rows(tq), o0_t, o1_t).T.astype(BF16)

    def run(with_latent):
        problems = []
        for u in range(n_sub):
            ql = slice(u * qk_lanes, (u + 1) * qk_lanes)
            kl = slice(0, qk_lanes) if shared_kv else ql
            vr = slice(0, LANES) if shared_kv else slice(u * LANES, (u + 1) * LANES)
            keys, vals = [kc_ref[:, kl]], [vtc[vr, :]]
            if with_latent:
                keys.append(kl_ref[:, kl])
                vals.append(vtl[vr, :])
            problems.append((_stack_heads(q_ref[:, ql]), keys, vals, None))
        for u, o_t in enumerate(_attend_t(problems)):
            finish(u, o_t)

    latent_queries = functools.partial(run, True)
    context_queries = functools.partial(run, False)

    if want_ctx:
        pl.when(t < nq)(latent_queries)
        pl.when(t >= nq)(context_queries)
    else:
        latent_queries()


def _q_row_map(rows, tq, want_ctx):
    nq = rows.S // tq
    nc = rows.C // tq
    B = rows.B

    def qrow(b, t):
        if not want_ctx:
            return b * nq + t
        return jnp.where(t < nq, b * nq + t, B * nq + b * nc + (t - nq))

    return qrow, nq, nc


def _attention(rows, tq, want_ctx, mode, q_src, k_src, v_src, n_groups, qk_lanes=LANES, shared_kv=False,
               extra=(), lam_init=0.0, name="attn"):
    B, S, C = rows.B, rows.S, rows.C
    n_sub = PAIRS_PER_STEP
    qrow, nq, nc = _q_row_map(rows, tq, want_ctx)
    n_q_tiles = nq + (nc if want_ctx else 0)
    ctx_blk0 = rows.n_lat // C
    (qarr, q0), (karr, k0), (varr, v0) = q_src, k_src, v_src
    qw = n_sub * qk_lanes
    kw = qk_lanes if shared_kv else qw
    vw = LANES if shared_kv else n_sub * LANES
    assert n_groups % n_sub == 0 and (q0 * qk_lanes) % qw == 0 and (k0 * qk_lanes) % kw == 0 and (v0 * LANES) % vw == 0
    qb, kb, vb = q0 * qk_lanes // qw, k0 * qk_lanes // kw, v0 * LANES // vw
    kstep = 0 if shared_kv else 1
    in_specs = [pl.BlockSpec((tq, qw), lambda b, g, t: (qrow(b, t), qb + g)),
                pl.BlockSpec((C, kw), lambda b, g, t: (ctx_blk0 + b, kb + kstep * g)),
                pl.BlockSpec((S, kw), lambda b, g, t: (b, kb + kstep * g)),
                pl.BlockSpec((C, vw), lambda b, g, t: (ctx_blk0 + b, vb + kstep * g)),
                pl.BlockSpec((S, vw), lambda b, g, t: (b, vb + kstep * g))]
    args = [qarr, karr, karr, varr, varr]
    for e in extra:
        in_specs.append(pl.BlockSpec(e.shape, lambda b, g, t: (0, 0)))
        args.append(e)
    n_out_rows = rows.n_all if want_ctx else rows.n_lat
    return pl.pallas_call(
        functools.partial(_attn_kernel, mode=mode, nq=nq, want_ctx=want_ctx, lam_init=lam_init, n_sub=n_sub,
                          qk_lanes=qk_lanes, shared_kv=shared_kv),
        grid=(B, n_groups // n_sub, n_q_tiles),
        in_specs=in_specs,
        out_specs=pl.BlockSpec((tq, n_sub * LANES), lambda b, g, t: (qrow(b, t), g)),
        out_shape=jax.ShapeDtypeStruct((n_out_rows, n_groups * LANES), BF16),
        scratch_shapes=[pltpu.VMEM((vw, C), BF16), pltpu.VMEM((vw, S), BF16)],
        compiler_params=_params(("parallel", "parallel", "arbitrary")),
        name=name,
    )(*args)


NBR_Q_ROWS = 4
NBR_WIN_ROWS = NBR_Q_ROWS + NA_KH


def _nbr_kernel(q_ref, kc_ref, kl_ref, vc_ref, vl_ref, bias_ref, o_ref, vtc, *, n_row_blocks, grid_rows, want_ctx):
    t = pl.program_id(2)
    tq = q_ref.shape[0]
    n_win = NBR_WIN_ROWS * GRID_W
    n_pairs = q_ref.shape[1] // LANES

    @pl.when(t == 0)
    def _():
        vtc[...] = _transpose_bf16(vc_ref[...])

    def run(with_window):
        if with_window:
            base = jnp.clip(t * NBR_Q_ROWS - NA_KH // 2, 0, grid_rows - NBR_WIN_ROWS)
            off = pl.multiple_of(base * GRID_W, GRID_W)
            vtw = _transpose_bf16(vl_ref[pl.ds(off, n_win), :])
        problems = []
        for u in range(n_pairs):
            ls = slice(u * LANES, (u + 1) * LANES)
            keys, vals, biases = [kc_ref[:, ls]], [vtc[ls, :]], None
            if with_window:
                keys.append(kl_ref[pl.ds(off, n_win), ls])
                vals.append(vtw[ls, :])
                biases = [None, bias_ref[0, u]]
            problems.append((_stack_heads(q_ref[:, ls]), keys, vals, biases))
        for u, o_t in enumerate(_attend_t(problems)):
            o_ref[:, u * LANES:(u + 1) * LANES] = jnp.where(_lo_rows(tq), o_t[:, :tq], o_t[:, tq:]).T.astype(BF16)

    latent_queries = functools.partial(run, True)
    context_queries = functools.partial(run, False)

    if want_ctx:
        pl.when(t < n_row_blocks)(latent_queries)
        pl.when(t >= n_row_blocks)(context_queries)
    else:
        latent_queries()


def _nbr_bias(rpb):
    kw = NA_KW
    n_h = rpb.shape[0]
    col = jnp.arange(GRID_W)
    col_start = jnp.clip(col - kw // 2, 0, GRID_W - kw)
    in_win = (col[None, :] >= col_start[:, None]) & (col[None, :] < col_start[:, None] + kw)
    dc = jnp.clip(col[None, :] - col[:, None], -(kw - 1), kw - 1) + (NA_KW - 1)
    onehot = (dc[:, :, None] == jnp.arange(2 * NA_KW - 1)[None, None, :]).astype(F32)
    by_col = jnp.einsum("hrc,qkc->hrqk", rpb.astype(F32) * LOG2E, onehot, precision=HIGHEST)
    g = jnp.arange(NBR_Q_ROWS)[:, None]
    j = jnp.arange(NBR_WIN_ROWS)[None, :]
    first_row = jnp.stack([jnp.zeros_like(g), g, jnp.full_like(g, NBR_Q_ROWS)])
    rel_off = jnp.array([NA_KH - 1, NA_KH - 1 - NBR_Q_ROWS, NA_KH - 1 - 2 * NBR_Q_ROWS])[:, None, None]
    dr = j[None] - g[None] + rel_off
    row_ok = (j[None] >= first_row) & (j[None] < first_row + NA_KH)
    picked = jnp.take(by_col, jnp.clip(dr, 0, 2 * NA_KH - 2).reshape(-1), axis=1)
    picked = picked.reshape(n_h, 3, NBR_Q_ROWS, NBR_WIN_ROWS, GRID_W, GRID_W)
    ok = row_ok[None, :, :, :, None, None] & in_win[None, None, None, None, :, :]
    bias = jnp.where(ok, picked, MASKED).reshape(n_h // 2, 2, 3, NBR_Q_ROWS, NBR_WIN_ROWS, GRID_W, GRID_W)
    bias = bias.transpose(2, 0, 4, 6, 1, 3, 5)
    return bias.reshape(3, n_h // 2, NBR_WIN_ROWS * GRID_W, 2 * NBR_Q_ROWS * GRID_W).astype(BF16)


def _nbr_attention(rows, want_ctx, misc, bias, q_blk0, k_blk0, v_blk0):
    B, S, C = rows.B, rows.S, rows.C
    tq = NBR_Q_ROWS * GRID_W
    grid_rows = S // GRID_W
    assert grid_rows >= NBR_WIN_ROWS and grid_rows % NBR_Q_ROWS == 0 and C % tq == 0
    qrow, nq, nc = _q_row_map(rows, tq, want_ctx)
    n_q_tiles = nq + (nc if want_ctx else 0)
    ctx_blk0 = rows.n_lat // C
    n_groups = B_HEADS // 2
    n_out_rows = rows.n_all if want_ctx else rows.n_lat
    n_keys = NBR_WIN_ROWS * GRID_W

    width = n_groups * LANES
    assert q_blk0 % n_groups == 0 and k_blk0 % n_groups == 0 and v_blk0 % n_groups == 0
    qb, kb, vb = q_blk0 // n_groups, k_blk0 // n_groups, v_blk0 // n_groups

    def bias_map(g, b, t):
        return (jnp.where(t == 0, 0, jnp.where(t < nq - 1, 1, 2)), 0, 0, 0)

    return pl.pallas_call(
        functools.partial(_nbr_kernel, n_row_blocks=nq, grid_rows=grid_rows, want_ctx=want_ctx),
        grid=(1, B, n_q_tiles),
        in_specs=[pl.BlockSpec((tq, width), lambda g, b, t: (qrow(b, t), qb)),
                  pl.BlockSpec((C, width), lambda g, b, t: (ctx_blk0 + b, kb)),
                  pl.BlockSpec((S, width), lambda g, b, t: (b, kb)),
                  pl.BlockSpec((C, width), lambda g, b, t: (ctx_blk0 + b, vb)),
                  pl.BlockSpec((S, width), lambda g, b, t: (b, vb)),
                  pl.BlockSpec((1, n_groups, n_keys, 2 * tq), bias_map)],
        out_specs=pl.BlockSpec((tq, width), lambda g, b, t: (qrow(b, t), 0)),
        out_shape=jax.ShapeDtypeStruct((n_out_rows, width), BF16),
        scratch_shapes=[pltpu.VMEM((width, C), BF16)],
        compiler_params=_params(("parallel", "parallel", "arbitrary")),
        name="attn_nbr",
    )(misc, misc, misc, misc, misc, bias)


def _residual_epilogue(x, y, gate, g_post, nxt):
    x_new = x + gate * _rms(y, g_post)
    h = None
    if nxt is not None:
        g_pre, sc, sh = nxt
        h = _modulated_norm(x_new, g_pre, sc, sh)
    return x_new, h


def _dot_split(a, b):
    a_hi, b_hi = a.astype(BF16), b.astype(BF16)
    a_lo = (a - a_hi.astype(F32)).astype(BF16)
    b_lo = (b - b_hi.astype(F32)).astype(BF16)
    dot = functools.partial(jnp.dot, preferred_element_type=F32)
    return dot(a_hi, b_hi) + (dot(a_hi, b_lo) + dot(a_lo, b_hi))


def _route(h, w_router):
    logits = _dot_split(h, w_router)
    n_e = logits.shape[1]
    idx = lax.broadcasted_iota(jnp.int32, logits.shape, 1)
    top1 = logits.max(axis=-1, keepdims=True)
    i1 = jnp.where(logits == top1, idx, n_e).min(axis=-1, keepdims=True)
    rest = jnp.where(idx == i1, -jnp.inf, logits)
    top2 = rest.max(axis=-1, keepdims=True)
    i2 = jnp.where(rest == top2, idx, n_e).min(axis=-1, keepdims=True)
    e2 = jnp.exp(top2 - top1)
    w1 = 1.0 / (1.0 + e2)
    w2 = e2 / (1.0 + e2)
    k = lax.broadcasted_iota(jnp.int32, (logits.shape[0], 2), 1)
    return jnp.where(k == 0, i1, i2), jnp.where(k == 0, w1, w2)


def _merge_kernel(gates_ref, oa_ref, ob_ref, oc_ref, od_ref, wbr_ref, wout_ref, x_ref, gate_ref, gpost_ref,
                  gpre_ref, sc_ref, sh_ref, *rest, routed):
    m = None
    for j, o_ref in enumerate((oa_ref, ob_ref, oc_ref, od_ref)):
        br = jnp.dot(o_ref[...], wbr_ref[j], preferred_element_type=F32)
        term = gates_ref[:, j * D_MODEL:(j + 1) * D_MODEL].astype(F32) * br
        m = term if m is None else m + term
    y = jnp.dot(m.astype(BF16), wout_ref[...], preferred_element_type=F32)
    x_new, h = _residual_epilogue(x_ref[...], y, gate_ref[0], gpost_ref[...],
                                  (gpre_ref[...], sc_ref[0], sh_ref[0]))
    if routed:
        wr_ref, x_out, h_out, idx_out, wts_out = rest
        idx_out[...], wts_out[...] = _route(h, wr_ref[...])
    else:
        x_out, h_out = rest
    x_out[...] = x_new
    h_out[...] = h.astype(h_out.dtype)


def _merge(rows, n_rows, gates, o_all, wbr, wout, x, mods, layer, g_post, g_pre, w_router, tm):
    row = lambda i: (i, 0)
    const2 = lambda i: (0, 0)
    routed = w_router is not None
    in_specs = [pl.BlockSpec((tm, N_BRANCH * D_MODEL), row)]
    in_specs += [pl.BlockSpec((tm, o.shape[1]), row) for o in o_all]
    in_specs += [pl.BlockSpec(wbr.shape, lambda i: (0, 0, 0)), pl.BlockSpec(wout.shape, const2),
                 pl.BlockSpec((tm, D_MODEL), row), rows.mod_spec(layer, 2, tm),
                 pl.BlockSpec((1, D_MODEL), const2), pl.BlockSpec((1, D_MODEL), const2),
                 rows.mod_spec(layer, 4, tm), rows.mod_spec(layer, 3, tm)]
    args = [gates, *o_all, wbr, wout, x, mods, g_post.reshape(1, -1), g_pre.reshape(1, -1), mods, mods]
    out_specs = [pl.BlockSpec((tm, D_MODEL), row), pl.BlockSpec((tm, D_MODEL), row)]
    out_shape = [jax.ShapeDtypeStruct((n_rows, D_MODEL), F32),
                 jax.ShapeDtypeStruct((n_rows, D_MODEL), F32 if routed else BF16)]
    if routed:
        in_specs.append(pl.BlockSpec(w_router.shape, const2))
        args.append(w_router)
        out_specs += [pl.BlockSpec((tm, 2), row), pl.BlockSpec((tm, 2), row)]
        out_shape += [jax.ShapeDtypeStruct((n_rows, 2), jnp.int32), jax.ShapeDtypeStruct((n_rows, 2), F32)]
    return pl.pallas_call(
        functools.partial(_merge_kernel, routed=routed),
        grid=(n_rows // tm,),
        in_specs=in_specs, out_specs=out_specs, out_shape=out_shape,
        compiler_params=_params(("parallel",)),
        name="merge_out",
    )(*args)


FFN_CHUNK = 256


def _ffn_kernel(h_ref, w1_ref, w3_ref, w2_ref, x_ref, gate_ref, gpost_ref, *rest, has_next):
    if has_next:
        gpre_ref, sc_ref, sh_ref, x_out, h_out = rest
    else:
        (x_out,) = rest
    h = h_ref[...]
    y = None
    for c0 in range(0, w1_ref.shape[1], FFN_CHUNK):
        cs = slice(c0, c0 + FFN_CHUNK)
        a = jnp.dot(h, w1_ref[:, cs], preferred_element_type=F32)
        b = jnp.dot(h, w3_ref[:, cs], preferred_element_type=F32)
        part = jnp.dot((a * jax.nn.sigmoid(a) * b).astype(BF16), w2_ref[cs, :], preferred_element_type=F32)
        y = part if y is None else y + part
    nxt = (gpre_ref[...], sc_ref[0], sh_ref[0]) if has_next else None
    x_new, hn = _residual_epilogue(x_ref[...], y, gate_ref[0], gpost_ref[...], nxt)
    x_out[...] = x_new
    if has_next:
        h_out[...] = hn.astype(BF16)


def _ffn(rows, n_rows, h, w1, w3, w2, x, mods, layer, g_post, g_pre_next, tm):
    row = lambda i: (i, 0)
    const2 = lambda i: (0, 0)
    has_next = g_pre_next is not None
    assert w1.shape[1] % FFN_CHUNK == 0
    resident = lambda a: pl.BlockSpec(a.shape, const2, pipeline_mode=pl.Buffered(1))
    in_specs = [pl.BlockSpec((tm, D_MODEL), row), resident(w1), resident(w3), resident(w2),
                pl.BlockSpec((tm, D_MODEL), row), rows.mod_spec(layer, 5, tm), pl.BlockSpec((1, D_MODEL), const2)]
    args = [h, w1, w3, w2, x, mods, g_post.reshape(1, -1)]
    out_specs = [pl.BlockSpec((tm, D_MODEL), row)]
    out_shape = [jax.ShapeDtypeStruct((n_rows, D_MODEL), F32)]
    if has_next:
        in_specs += [pl.BlockSpec((1, D_MODEL), const2), rows.mod_spec(layer + 1, 1, tm),
                     rows.mod_spec(layer + 1, 0, tm)]
        args += [g_pre_next.reshape(1, -1), mods, mods]
        out_specs.append(pl.BlockSpec((tm, D_MODEL), row))
        out_shape.append(jax.ShapeDtypeStruct((n_rows, D_MODEL), BF16))
    return pl.pallas_call(
        functools.partial(_ffn_kernel, has_next=has_next),
        grid=(n_rows // tm,),
        in_specs=in_specs, out_specs=out_specs, out_shape=out_shape,
        compiler_params=_params(("parallel",)),
        name="ffn_dense",
    )(*args)


def _expert_slots(idx, n_tok, tme):
    n_e = N_EXPERTS
    n_asg = 2 * n_tok
    e_flat = idx.reshape(-1)
    onehot = (e_flat[:, None] == jnp.arange(n_e, dtype=jnp.int32)[None, :]).astype(jnp.int32)
    csum = jnp.cumsum(onehot, axis=0)
    counts = csum[-1]
    group = ((counts + tme - 1) // tme) * tme
    group_end = jnp.cumsum(group)
    group_start = group_end - group
    slot = jnp.sum(onehot * (csum - 1 + group_start[None, :]), axis=1)
    n_tiles = n_asg // tme + n_e
    token_of_slot = jnp.zeros((n_tiles * tme,), jnp.int32).at[slot].set(
        jnp.arange(n_asg, dtype=jnp.int32) // 2, unique_indices=True)
    tile_start = jnp.arange(n_tiles, dtype=jnp.int32) * tme
    tile_e = jnp.minimum(jnp.sum((tile_start[:, None] >= group_end[None, :]).astype(jnp.int32), axis=1), n_e - 1)
    sel = (tile_e[:, None] == jnp.arange(n_e, dtype=jnp.int32)[None, :]).astype(jnp.int32)
    filled_end = jnp.sum(sel * (group_start + counts)[None, :], axis=1)
    tile_rows = jnp.clip(filled_end - tile_start, 0, tme)
    return slot.astype(jnp.int32), token_of_slot, tile_e.astype(jnp.int32), tile_rows.astype(jnp.int32)


GATHER_BLOCK = 256
GATHER_UNROLL = 8


def _row_gather(src_hbm, dst, sem, index_of_row, n_blocks):
    def block(blk, carry):
        def body(i, c):
            r = blk * GATHER_BLOCK + i
            src_row = 0 if index_of_row is None else index_of_row(r)
            cp = pltpu.make_async_copy(src_hbm.at[pl.ds(src_row, 1), :], dst.at[pl.ds(r, 1), :], sem)
            if index_of_row is None:
                cp.wait()
            else:
                cp.start()
            return c

        return lax.fori_loop(0, GATHER_BLOCK, body, carry, unroll=GATHER_UNROLL)

    lax.fori_loop(0, n_blocks, block, 0)


def _moe_ffn_kernel(tile_e, tile_rows, tok, h_hbm, w1_ref, w3_ref, w2_ref, y_ref, xg, xb, acc, sem, *, tme, sub):
    t = pl.program_id(0)
    f = pl.program_id(1)
    n_t = pl.num_programs(0)
    buf = t % 2

    def gather_blocks(tile):
        return (tile_rows[tile] + sub - 1) // sub

    def start(tile, b):
        _row_gather(h_hbm, xg.at[b], sem.at[b], lambda r: tok[tile * tme + r], gather_blocks(tile))

    @pl.when(f == 0)
    def _():
        @pl.when(t == 0)
        def _():
            start(0, 0)

        @pl.when(t + 1 < n_t)
        def _():
            start(t + 1, 1 - buf)

        _row_gather(h_hbm, xg.at[buf], sem.at[buf], None, gather_blocks(t))
        acc[...] = jnp.zeros_like(acc)

    def swiglu_rows(rs):
        @pl.when(f == 0)
        def _():
            xb[rs, :] = xg[buf, rs, :].astype(BF16)

        x = xb[rs, :]
        a = jnp.dot(x, w1_ref[0].astype(BF16), preferred_element_type=F32)
        b = jnp.dot(x, w3_ref[0].astype(BF16), preferred_element_type=F32)
        acc[rs, :] += jnp.dot((a * jax.nn.sigmoid(a) * b).astype(BF16), w2_ref[0].astype(BF16),
                              preferred_element_type=F32)

    full = tile_rows[t] == tme

    @pl.when(full)
    def _():
        swiglu_rows(slice(0, tme))

    for sb in range(tme // sub):
        @pl.when(jnp.logical_and(jnp.logical_not(full), sb * sub < tile_rows[t]))
        def _():
            swiglu_rows(slice(sb * sub, (sb + 1) * sub))

    @pl.when(f == pl.num_programs(1) - 1)
    def _():
        y_ref[...] = acc[...]


def _moe_ffn(h, token_of_slot, tile_e, tile_rows, w1, w3, w2, tme, tf):
    n_e, d, ff = w1.shape
    n_tiles = tile_e.shape[0]
    sub = GATHER_BLOCK
    grid_spec = pltpu.PrefetchScalarGridSpec(
        num_scalar_prefetch=3,
        grid=(n_tiles, ff // tf),
        in_specs=[pl.BlockSpec(memory_space=pl.ANY),
                  pl.BlockSpec((1, d, tf), lambda t, f, te, tr, tok: (te[t], 0, f)),
                  pl.BlockSpec((1, d, tf), lambda t, f, te, tr, tok: (te[t], 0, f)),
                  pl.BlockSpec((1, tf, d), lambda t, f, te, tr, tok: (te[t], f, 0))],
        out_specs=pl.BlockSpec((tme, d), lambda t, f, te, tr, tok: (t, 0)),
        scratch_shapes=[pltpu.VMEM((2, tme, d), F32), pltpu.VMEM((tme, d), BF16), pltpu.VMEM((tme, d), F32),
                        pltpu.SemaphoreType.DMA((2,))],
    )
    return pl.pallas_call(
        functools.partial(_moe_ffn_kernel, tme=tme, sub=sub),
        grid_spec=grid_spec,
        out_shape=jax.ShapeDtypeStruct((n_tiles * tme, d), F32),
        compiler_params=_params(("arbitrary", "arbitrary")),
        name="ffn_moe",
    )(tile_e, tile_rows, token_of_slot, h, w1, w3, w2)


def _moe_combine_kernel(slot, y_hbm, wts_ref, x_ref, gate_ref, gpost_ref, *rest, tm, has_next):
    if has_next:
        gpre_ref, sc_ref, sh_ref, x_out, h_out, yb, sem = rest
    else:
        x_out, yb, sem = rest
    i = pl.program_id(0)
    n_i = pl.num_programs(0)
    buf = i % 2

    def start(tile, b):
        for k in range(2):
            _row_gather(y_hbm, yb.at[b, k], sem.at[b], lambda r, k=k: slot[2 * (tile * tm + r) + k], tm // GATHER_BLOCK)

    @pl.when(i == 0)
    def _():
        start(0, 0)

    @pl.when(i + 1 < n_i)
    def _():
        start(i + 1, 1 - buf)

    for k in range(2):
        _row_gather(y_hbm, yb.at[buf, k], sem.at[buf], None, tm // GATHER_BLOCK)
    wts = wts_ref[...]
    y = wts[:, 0:1] * yb[buf, 0] + wts[:, 1:2] * yb[buf, 1]
    nxt = (gpre_ref[...], sc_ref[0], sh_ref[0]) if has_next else None
    x_new, hn = _residual_epilogue(x_ref[...], y, gate_ref[0], gpost_ref[...], nxt)
    x_out[...] = x_new
    if has_next:
        h_out[...] = hn.astype(BF16)


def _moe_combine(rows, n_rows, slot, y, wts, x, mods, layer, g_post, g_pre_next, tm):
    has_next = g_pre_next is not None
    row = lambda i, s: (i, 0)
    const2 = lambda i, s: (0, 0)

    mod_spec = lambda lyr, k: rows.mod_spec(lyr, k, tm)
    in_specs = [pl.BlockSpec(memory_space=pl.ANY), pl.BlockSpec((tm, 2), row),
                pl.BlockSpec((tm, D_MODEL), row), mod_spec(layer, 5), pl.BlockSpec((1, D_MODEL), const2)]
    args = [y, wts, x, mods, g_post.reshape(1, -1)]
    out_specs = [pl.BlockSpec((tm, D_MODEL), row)]
    out_shape = [jax.ShapeDtypeStruct((n_rows, D_MODEL), F32)]
    if has_next:
        in_specs += [pl.BlockSpec((1, D_MODEL), const2), mod_spec(layer + 1, 1), mod_spec(layer + 1, 0)]
        args += [g_pre_next.reshape(1, -1), mods, mods]
        out_specs.append(pl.BlockSpec((tm, D_MODEL), row))
        out_shape.append(jax.ShapeDtypeStruct((n_rows, D_MODEL), BF16))
    grid_spec = pltpu.PrefetchScalarGridSpec(
        num_scalar_prefetch=1, grid=(n_rows // tm,), in_specs=in_specs, out_specs=out_specs,
        scratch_shapes=[pltpu.VMEM((2, 2, tm, D_MODEL), F32), pltpu.SemaphoreType.DMA((2,))])
    return pl.pallas_call(
        functools.partial(_moe_combine_kernel, tm=tm, has_next=has_next),
        grid_spec=grid_spec, out_shape=out_shape,
        compiler_params=_params(("arbitrary",)),
        name="moe_combine",
    )(slot, *args)


def _in_proj_weights(w_in):
    widths = (512, 512, 512, 512, 512, 512, 512, 128, 128, MLA_Q_LORA, MLA_KV_LORA, MLA_ROPE, N_BRANCH * D_MODEL)
    cuts, o = [], 0
    for w in widths:
        cuts.append((o, o + w))
        o += w
    aq, ak, av, bq, bk, bv, cq, ck, cv, dqa, dkva, dkr, gates = [w_in[:, a:b] for a, b in cuts]
    cq = cq.reshape(-1, 2, 4, C_HEAD_DIM).transpose(0, 2, 1, 3).reshape(-1, C_HEADS * C_HEAD_DIM)
    dkr_pad = jnp.zeros((w_in.shape[0], LANES), w_in.dtype).at[:, MLA_NOPE:MLA_NOPE + MLA_ROPE].set(dkr)
    w1 = jnp.concatenate([aq * (A_QK_DIM ** -0.5 * LOG2E), ak], axis=1)
    w2 = jnp.concatenate([av, bq * (B_HEAD_DIM ** -0.5 * LOG2E), bk, bv], axis=1)
    w3 = jnp.concatenate([cq, ck, cv], axis=1)
    w5 = jnp.concatenate([dqa, dkva, dkr_pad], axis=1)
    return [w.astype(BF16) for w in (w1, w2, w3, gates, w5)]


def _mla_weights(w_uq, w_ukv):
    dq = MLA_NOPE + MLA_ROPE
    wuq = jnp.pad(w_uq.reshape(-1, D_HEADS, dq), ((0, 0), (0, 0), (0, LANES - dq))).reshape(-1, D_HEADS * LANES)
    kv = w_ukv.reshape(-1, D_HEADS, MLA_NOPE + MLA_V)
    wuk = jnp.pad(kv[:, :, :MLA_NOPE], ((0, 0), (0, 0), (0, LANES - MLA_NOPE))).reshape(-1, D_HEADS * LANES)
    wuv = kv[:, :, MLA_NOPE:].reshape(-1, D_HEADS * MLA_V)
    return [wuq.astype(BF16), wuk.astype(BF16), wuv.astype(BF16)]


def kernel(x, c, ctx, c_ctx, w_ada, b_ada, g_mix_pre, g_mix_post, g_ffn_pre, g_ffn_post, w_in, lam_q1, lam_k1, lam_q2, lam_k2, g_diff_sub, na_rpb, g_qnorm, g_knorm, g_q_lora, w_uq, g_kv_lora, w_ukv, w_br_a, w_br_b, w_br_c, w_br_d, w_out, w1_dense, w3_dense, w2_dense, w_router, w1_moe, w3_moe, w2_moe):
    B, S, D = x.shape
    C = ctx.shape[1]
    depth = w_in.shape[0]
    assert D == D_MODEL and B + 1 <= MOD_ROWS
    rows = _Rows(B, S, C)
    tq = 256
    tm_proj = _pick_tile((1024, 512, 256), S, B * C)
    tm_row = _pick_tile((512, 256), S, B * C)

    cvec = jnp.zeros((MOD_ROWS, D), F32).at[:B].set(c).at[B].set(c_ctx)
    mods = _mods(cvec, w_ada, b_ada)

    rope_qk = _rope_tables(S, A_QK_DIM, 0, tm_proj)
    rope_mla = _rope_tables(S, MLA_ROPE, MLA_NOPE, tm_proj)

    xs = jnp.concatenate([x.reshape(B * S, D), ctx.reshape(B * C, D)], axis=0)
    h = _prenorm(rows, xs, g_mix_pre[0], mods, 0, tm_proj)

    for i in range(depth):
        last = i == depth - 1
        want_ctx = not last
        n_rows = rows.n_all if want_ctx else rows.n_lat
        lam_init = 0.8 - 0.6 * math.exp(-0.3 * i)

        gain_c = jnp.concatenate([jnp.tile(g_qnorm[i], C_HEADS) * (C_HEAD_DIM ** -0.5 * LOG2E),
                                  jnp.tile(g_knorm[i], C_KV_HEADS)]).reshape(1, -1).astype(F32)
        proj_weights = _in_proj_weights(w_in[i]) + _mla_weights(w_uq[i], w_ukv[i])
        qka, misc, qkc, gates, qd, kd, vd = _in_proj(rows, h, proj_weights, rope_qk, rope_mla, gain_c,
                                                     g_q_lora[i], g_kv_lora[i], tm_row)

        lam_vecs = [v[i].reshape(1, -1).astype(F32) for v in (lam_q1, lam_k1, lam_q2, lam_k2)]
        oa = _attention(rows, tq, want_ctx, "diff",
                        q_src=(qka, 0), k_src=(qka, A_HEADS), v_src=(misc, 0), n_groups=A_HEADS,
                        extra=lam_vecs + [g_diff_sub[i].reshape(1, -1).astype(F32)], lam_init=lam_init,
                        name="attn_diff")
        ob = _nbr_attention(rows, want_ctx, misc, _nbr_bias(na_rpb[i]), q_blk0=4, k_blk0=8, v_blk0=12)
        oc = _attention(rows, tq, want_ctx, "select",
                        q_src=(qkc, 0), k_src=(qkc, 4), v_src=(qkc, 5), n_groups=C_HEADS // 2, shared_kv=True,
                        name="attn_gqa")
        od = _attention(rows, tq, want_ctx, "select",
                        q_src=(qd, 0), k_src=(kd, 0), v_src=(vd, 0),
                        n_groups=D_HEADS // 2, qk_lanes=2 * LANES, name="attn_mla")

        wbc = w_br_c[i].reshape(2, 4, C_HEAD_DIM, D).transpose(1, 0, 2, 3).reshape(C_HEADS * C_HEAD_DIM, D)
        wbr = jnp.stack([w_br_a[i], w_br_b[i], wbc, w_br_d[i]]).astype(BF16)
        moe = i % 2 == 1
        j = i // 2
        merged = _merge(rows, n_rows, gates, (oa, ob, oc, od), wbr, w_out[i].astype(BF16), xs, mods, i,
                        g_mix_post[i], g_ffn_pre[i], w_router[j] if moe else None, tm_row)
        g_pre_next = None if last else g_mix_pre[i + 1]
        if moe:
            xs, h2, route_idx, route_wts = merged
            slot, token_of_slot, tile_e, tile_rows = _expert_slots(route_idx, n_rows, MOE_TILE)
            y = _moe_ffn(h2, token_of_slot, tile_e, tile_rows, w1_moe[j], w3_moe[j], w2_moe[j], MOE_TILE, 512)
            outs = _moe_combine(rows, n_rows, slot, y, route_wts, xs, mods, i, g_ffn_post[i], g_pre_next, 256)
        else:
            xs, h2 = merged
            outs = _ffn(rows, n_rows, h2, w1_dense[j].astype(BF16), w3_dense[j].astype(BF16),
                        w2_dense[j].astype(BF16), xs, mods, i, g_ffn_post[i], g_pre_next, tm_row)
        if last:
            xs = outs[0]
        else:
            xs, h = outs
    return xs[:B * S].reshape(B, S, D)
```

```python
import functools
import math

import jax
import jax.numpy as jnp
from jax import lax
from jax.experimental import pallas as pl
from jax.experimental.pallas import tpu as pltpu

F32 = jnp.float32
BF16 = jnp.bfloat16
HIGHEST = lax.Precision.HIGHEST

D_MODEL = 1024
GRID_W = 64
ROPE_BASE = 10000.0
EPS = 1e-6
N_BRANCH = 4
A_HEADS = 4
A_QK_DIM = 64
A_V_DIM = 128
B_HEADS = 8
B_HEAD_DIM = 64
NA_KH = 8
NA_KW = 16
C_HEADS = 8
C_KV_HEADS = 2
C_HEAD_DIM = 64
D_HEADS = 8
MLA_Q_LORA = 256
MLA_KV_LORA = 128
MLA_NOPE = 64
MLA_ROPE = 32
MLA_V = 64
N_EXPERTS = 8

LANES = 128
HALF = LANES // 2
MASKED = -1e30
LOG2E = math.log2(math.e)
PAIRS_PER_STEP = 4
MERGE_SPLIT = 2
N_MOD = 6
MOD_ROWS = 16
MOE_TILE = 1024
MOE_F_TILE = 512
ATTN_Q_TILE = 256
VMEM_V7X = 64 * 1024 * 1024
VMEM_LIMIT = VMEM_V7X * 7 // 8

NT_DIMS = (((1,), (1,)), ((), ()))


def _pick_tile(candidates, *dims):
    for t in candidates:
        if all(d % t == 0 for d in dims):
            return t
    raise ValueError(f"no tile in {candidates} divides {dims}")


def _params(sem):
    return pltpu.CompilerParams(dimension_semantics=sem, vmem_limit_bytes=VMEM_LIMIT)


def _rms(x, g):
    return x * lax.rsqrt(jnp.mean(x * x, axis=-1, keepdims=True) + EPS) * g


def _modulated_norm(x, g_pre, sc, sh):
    return _rms(x, g_pre) * (1.0 + sc) + sh


def _rope(x, cos, sin_a, sin_b, half):
    up = pltpu.roll(x, LANES - half, axis=1)
    dn = pltpu.roll(x, half, axis=1)
    return x * cos + up * sin_a + dn * sin_b


def _group_mean_sq(x):
    r = lax.broadcasted_iota(jnp.int32, (LANES, LANES), 0) // HALF
    c = lax.broadcasted_iota(jnp.int32, (LANES, LANES), 1) // HALF
    ones = jnp.where(r == c, 1.0, 0.0).astype(BF16)
    sq = x * x
    hi = sq.astype(BF16)
    lo = (sq - hi.astype(F32)).astype(BF16)
    ss = jnp.dot(hi, ones, preferred_element_type=F32) + jnp.dot(lo, ones, preferred_element_type=F32)
    return ss * (1.0 / HALF)


def _rope_tables(seq_len, d_rot, lane_off, pad_rows):
    t = jnp.arange(seq_len)
    row = (t // GRID_W).astype(F32)
    col = (t % GRID_W).astype(F32)
    d_ax = d_rot // 2
    half = d_ax // 2
    inv = ROPE_BASE ** (-jnp.arange(0, d_ax, 2, dtype=F32) / d_ax)
    ang_r = row[:, None] * inv[None]
    ang_c = col[:, None] * inv[None]
    ang = jnp.concatenate([ang_r, ang_r, ang_c, ang_c], axis=-1)
    cos, sin = jnp.cos(ang), jnp.sin(ang)
    first = (jnp.arange(d_rot) % d_ax) < half
    sin_a = jnp.where(first[None], -sin, 0.0)
    sin_b = jnp.where(first[None], 0.0, sin)
    reps = (LANES - lane_off) // d_rot if lane_off == 0 else 1

    def place(tab, fill):
        tab = jnp.tile(tab, (1, reps))
        full = jnp.full((seq_len, LANES), fill, F32)
        full = full.at[:, lane_off:lane_off + tab.shape[1]].set(tab)
        return jnp.concatenate([full, jnp.full((pad_rows, LANES), fill, F32)], axis=0)

    return place(cos, 1.0), place(sin_a, 0.0), place(sin_b, 0.0), half


def _mods_kernel(c_ref, w_ref, b_ref, o_ref):
    c = c_ref[...]
    s = c * jax.nn.sigmoid(c)
    o_ref[0] = lax.dot_general(s, w_ref[0], (((1,), (0,)), ((), ())), precision=HIGHEST,
                               preferred_element_type=F32) + b_ref[0]


def _mods(cvec, w_ada, b_ada):
    n_layers, d, n6 = w_ada.shape
    tn = n6 // 4
    out = pl.pallas_call(
        _mods_kernel,
        grid=(n_layers, n6 // tn),
        in_specs=[pl.BlockSpec((MOD_ROWS, d), lambda l, j: (0, 0)),
                  pl.BlockSpec((1, d, tn), lambda l, j: (l, 0, j)),
                  pl.BlockSpec((1, 1, tn), lambda l, j: (l, 0, j))],
        out_specs=pl.BlockSpec((1, MOD_ROWS, tn), lambda l, j: (l, 0, j)),
        out_shape=jax.ShapeDtypeStruct((n_layers, MOD_ROWS, n6), F32),
        compiler_params=_params(("arbitrary", "arbitrary")),
        name="adaln_mods",
    )(cvec, w_ada, b_ada.reshape(n_layers, 1, n6))
    return out.reshape(n_layers * MOD_ROWS * N_MOD, 1, d)


class _Rows:
    def __init__(self, batch, seq, ctx_len):
        self.B, self.S, self.C = batch, seq, ctx_len
        self.n_lat = batch * seq
        self.n_all = batch * (seq + ctx_len)

    def mod_spec(self, layer, k, tm):
        n_lat_tiles = self.n_lat // tm
        per = self.S // tm
        B = self.B

        def imap(i, *_):
            brow = jnp.where(i < n_lat_tiles, i // per, B)
            return ((layer * MOD_ROWS + brow) * N_MOD + k, 0, 0)

        return pl.BlockSpec((1, 1, D_MODEL), imap)


def _prenorm_kernel(xl_ref, xc_ref, g_ref, sc_ref, sh_ref, xs_ref, h_ref, *, n_lat_tiles):
    i = pl.program_id(0)

    def emit(x):
        xs_ref[...] = x
        h_ref[...] = _modulated_norm(x, g_ref[...], sc_ref[0], sh_ref[0]).astype(BF16)

    pl.when(i < n_lat_tiles)(lambda: emit(xl_ref[...]))
    pl.when(i >= n_lat_tiles)(lambda: emit(xc_ref[...]))


def _prenorm(rows, x_lat, x_ctx, g_pre, mods, layer, tm):
    n_lat_tiles = rows.n_lat // tm
    n_ctx_tiles = (rows.n_all - rows.n_lat) // tm
    row = lambda i: (i, 0)
    return pl.pallas_call(
        functools.partial(_prenorm_kernel, n_lat_tiles=n_lat_tiles),
        grid=(n_lat_tiles + n_ctx_tiles,),
        in_specs=[pl.BlockSpec((tm, D_MODEL), lambda i: (jnp.minimum(i, n_lat_tiles - 1), 0)),
                  pl.BlockSpec((tm, D_MODEL), lambda i: (jnp.maximum(i - n_lat_tiles, 0), 0)),
                  pl.BlockSpec((1, D_MODEL), lambda i: (0, 0)),
                  rows.mod_spec(layer, 1, tm), rows.mod_spec(layer, 0, tm)],
        out_specs=[pl.BlockSpec((tm, D_MODEL), row), pl.BlockSpec((tm, D_MODEL), row)],
        out_shape=[jax.ShapeDtypeStruct((rows.n_all, D_MODEL), F32),
                   jax.ShapeDtypeStruct((rows.n_all, D_MODEL), BF16)],
        compiler_params=_params(("parallel",)),
        name="prenorm",
    )(x_lat, x_ctx, g_pre.reshape(1, -1), mods, mods)


PROJ_CHUNK = 512


def _proj_chunks(x, w_ref, o_ref, epilogue):
    n = w_ref.shape[1]
    for c0 in range(0, n, PROJ_CHUNK):
        c1 = min(c0 + PROJ_CHUNK, n)
        acc = jnp.dot(x, w_ref[:, c0:c1], preferred_element_type=F32)
        for s in range((c1 - c0) // LANES):
            col = c0 + s * LANES
            o_ref[:, col:col + LANES] = epilogue(acc[:, s * LANES:(s + 1) * LANES], col).astype(BF16)


def _in_proj_kernel(h_ref, w1_ref, w2_ref, w3_ref, w4_ref, w5_ref, wuq_ref, wuk_ref, wuv_ref,
                    cos_ref, sa_ref, sb_ref, cosd_ref, sad_ref, sbd_ref, gc_ref, gq_ref, gkv_ref,
                    qka_ref, misc_ref, qkc_ref, gates_ref, qd_ref, kd_ref, vd_ref, *, half_qk, half_mla, mla_scale):
    h = h_ref[...]
    n_norm = gc_ref.shape[1]

    def rope_qk(blk):
        return _rope(blk, cos_ref[...], sa_ref[...], sb_ref[...], half_qk)

    def rope_mla(blk):
        return _rope(blk, cosd_ref[...], sad_ref[...], sbd_ref[...], half_mla)

    def head_norm(blk, col):
        if col >= n_norm:
            return blk
        return rope_qk(blk * lax.rsqrt(_group_mean_sq(blk) + EPS) * gc_ref[:, col:col + LANES])

    _proj_chunks(h, w1_ref, qka_ref, lambda blk, col: rope_qk(blk))
    _proj_chunks(h, w2_ref, misc_ref, lambda blk, col: blk)
    _proj_chunks(h, w3_ref, qkc_ref, head_norm)
    _proj_chunks(h, w4_ref, gates_ref, lambda blk, col: jax.nn.sigmoid(blk))

    low = jnp.dot(h, w5_ref[...], preferred_element_type=F32)
    qn = _rms(low[:, :MLA_Q_LORA], gq_ref[...]).astype(BF16)
    kvn = _rms(low[:, MLA_Q_LORA:MLA_Q_LORA + MLA_KV_LORA], gkv_ref[...]).astype(BF16)
    k_rope = rope_mla(low[:, MLA_Q_LORA + MLA_KV_LORA:])
    _proj_chunks(qn, wuq_ref, qd_ref, lambda blk, col: rope_mla(blk) * mla_scale)
    _proj_chunks(kvn, wuk_ref, kd_ref, lambda blk, col: blk + k_rope)
    _proj_chunks(kvn, wuv_ref, vd_ref, lambda blk, col: blk)


def _in_proj(rows, h, weights, rope_qk, rope_mla, gain_c, g_q, g_kv, tm):
    n, k = h.shape
    n_lat_tiles = rows.n_lat // tm
    per = rows.S // tm
    row = lambda i: (i, 0)
    const = lambda i: (0, 0)
    table = pl.BlockSpec((tm, LANES), lambda i: (jnp.where(i < n_lat_tiles, i % per, per), 0))
    resident = lambda a: pl.BlockSpec(a.shape, const, pipeline_mode=pl.Buffered(1))
    w1, w2, w3, w4, w5, wuq, wuk, wuv = weights
    out_widths = [w1.shape[1], w2.shape[1], w3.shape[1], w4.shape[1], wuq.shape[1], wuk.shape[1], wuv.shape[1]]
    vecs = [gain_c, g_q.reshape(1, -1).astype(F32), g_kv.reshape(1, -1).astype(F32)]
    return pl.pallas_call(
        functools.partial(_in_proj_kernel, half_qk=rope_qk[3], half_mla=rope_mla[3],
                          mla_scale=(MLA_NOPE + MLA_ROPE) ** -0.5 * LOG2E),
        grid=(n // tm,),
        in_specs=([pl.BlockSpec((tm, k), row)] + [resident(w) for w in weights] + [table] * 6
                  + [pl.BlockSpec(v.shape, const) for v in vecs]),
        out_specs=[pl.BlockSpec((tm, w), row) for w in out_widths],
        out_shape=[jax.ShapeDtypeStruct((n, w), BF16) for w in out_widths],
        compiler_params=_params(("parallel",)),
        name="in_proj",
    )(h, *weights, *rope_qk[:3], *rope_mla[:3], *vecs)


def _lo_rows(n):
    return lax.broadcasted_iota(jnp.int32, (LANES, n), 0) < HALF


def _stack_heads(q):
    half = q.shape[1] // 2
    lo = lax.broadcasted_iota(jnp.int32, q.shape, 1) < half
    zero = jnp.zeros_like(q)
    return jnp.concatenate([jnp.where(lo, q, zero), jnp.where(lo, zero, q)], axis=0)


def _transpose_bf16(v):
    return v.astype(F32).T.astype(BF16)


def _attend_t(problems):
    all_scores = []
    for w, keys, _, biases in problems:
        scores = []
        for idx, k in enumerate(keys):
            s = lax.dot_general(k, w, NT_DIMS, preferred_element_type=F32)
            if biases is not None and biases[idx] is not None:
                s = s + biases[idx].astype(F32)
            scores.append(s)
        all_scores.append(scores)
    outs = []
    for scores, (_, _, vals_t, _) in zip(all_scores, problems):
        m = scores[0].max(axis=0, keepdims=True)
        for s in scores[1:]:
            m = jnp.maximum(m, s.max(axis=0, keepdims=True))
        denom = None
        out = None
        for s, vt in zip(scores, vals_t):
            e = jnp.exp2(s - m)
            d = e.sum(axis=0, keepdims=True)
            o = jnp.dot(vt, e.astype(BF16), preferred_element_type=F32)
            denom = d if denom is None else denom + d
            out = o if out is None else out + o
        outs.append(out / denom)
    return outs


def _attn_kernel(q_ref, kc_ref, kl_ref, vc_ref, vl_ref, *rest, mode, nq, want_ctx, lam_init, n_sub, qk_lanes,
                 shared_kv):
    vtc, vtl = rest[-2:]
    o_ref = rest[-3]
    extra = rest[:-3]
    t = pl.program_id(2)
    tq = q_ref.shape[0]

    @pl.when(t == 0)
    def _():
        vtc[...] = _transpose_bf16(vc_ref[...])
        vtl[...] = _transpose_bf16(vl_ref[...])

    def finish(u, o_t):
        o0_t, o1_t = o_t[:, :tq], o_t[:, tq:]
        cols = slice(u * LANES, (u + 1) * LANES)
        if mode == "diff":
            lq1, lk1, lq2, lk2, g_ref = extra
            lam = (jnp.exp(jnp.sum(lq1[...] * lk1[...], axis=-1, keepdims=True))
                   - jnp.exp(jnp.sum(lq2[...] * lk2[...], axis=-1, keepdims=True)) + lam_init)
            o = (o0_t - lam * o1_t).T
            o_ref[:, cols] = (_rms(o, g_ref[...]) * (1.0 - lam_init)).astype(BF16)
        else:
            o_ref[:, cols] = jnp.where(_lo_rows(tq), o0_t, o1_t).T.astype(BF16)

    def run(with_latent):
        problems = []
        for u in range(n_sub):
            ql = slice(u * qk_lanes, (u + 1) * qk_lanes)
            kl = slice(0, qk_lanes) if shared_kv else ql
            vr = slice(0, LANES) if shared_kv else slice(u * LANES, (u + 1) * LANES)
            keys, vals = [kc_ref[:, kl]], [vtc[vr, :]]
            if with_latent:
                keys.append(kl_ref[:, kl])
                vals.append(vtl[vr, :])
            problems.append((_stack_heads(q_ref[:, ql]), keys, vals, None))
        for u, o_t in enumerate(_attend_t(problems)):
            finish(u, o_t)

    latent_queries = functools.partial(run, True)
    context_queries = functools.partial(run, False)

    if want_ctx:
        pl.when(t < nq)(latent_queries)
        pl.when(t >= nq)(context_queries)
    else:
        latent_queries()


def _q_row_map(rows, tq, want_ctx):
    nq = rows.S // tq
    nc = rows.C // tq
    B = rows.B

    def qrow(b, t):
        if not want_ctx:
            return b * nq + t
        return jnp.where(t < nq, b * nq + t, B * nq + b * nc + (t - nq))

    return qrow, nq, nc


def _attention(rows, tq, want_ctx, mode, q_src, k_src, v_src, n_groups, qk_lanes=LANES, shared_kv=False,
               extra=(), lam_init=0.0, name="attn"):
    B, S, C = rows.B, rows.S, rows.C
    n_sub = PAIRS_PER_STEP
    qrow, nq, nc = _q_row_map(rows, tq, want_ctx)
    n_q_tiles = nq + (nc if want_ctx else 0)
    ctx_blk0 = rows.n_lat // C
    (qarr, q0), (karr, k0), (varr, v0) = q_src, k_src, v_src
    qw = n_sub * qk_lanes
    kw = qk_lanes if shared_kv else qw
    vw = LANES if shared_kv else n_sub * LANES
    assert n_groups % n_sub == 0 and (q0 * qk_lanes) % qw == 0 and (k0 * qk_lanes) % kw == 0 and (v0 * LANES) % vw == 0
    qb, kb, vb = q0 * qk_lanes // qw, k0 * qk_lanes // kw, v0 * LANES // vw
    kstep = 0 if shared_kv else 1
    in_specs = [pl.BlockSpec((tq, qw), lambda b, g, t: (qrow(b, t), qb + g)),
                pl.BlockSpec((C, kw), lambda b, g, t: (ctx_blk0 + b, kb + kstep * g)),
                pl.BlockSpec((S, kw), lambda b, g, t: (b, kb + kstep * g)),
                pl.BlockSpec((C, vw), lambda b, g, t: (ctx_blk0 + b, vb + kstep * g)),
                pl.BlockSpec((S, vw), lambda b, g, t: (b, vb + kstep * g))]
    args = [qarr, karr, karr, varr, varr]
    for e in extra:
        in_specs.append(pl.BlockSpec(e.shape, lambda b, g, t: (0, 0)))
        args.append(e)
    n_out_rows = rows.n_all if want_ctx else rows.n_lat
    return pl.pallas_call(
        functools.partial(_attn_kernel, mode=mode, nq=nq, want_ctx=want_ctx, lam_init=lam_init, n_sub=n_sub,
                          qk_lanes=qk_lanes, shared_kv=shared_kv),
        grid=(B, n_groups // n_sub, n_q_tiles),
        in_specs=in_specs,
        out_specs=pl.BlockSpec((tq, n_sub * LANES), lambda b, g, t: (qrow(b, t), g)),
        out_shape=jax.ShapeDtypeStruct((n_out_rows, n_groups * LANES), BF16),
        scratch_shapes=[pltpu.VMEM((vw, C), BF16), pltpu.VMEM((vw, S), BF16)],
        compiler_params=_params(("parallel", "parallel", "arbitrary")),
        name=name,
    )(*args)


NBR_Q_ROWS = 4
NBR_WIN_ROWS = NBR_Q_ROWS + NA_KH


def _nbr_kernel(q_ref, kc_ref, kl_ref, vc_ref, vl_ref, bias_ref, o_ref, vtc, *, n_row_blocks, grid_rows, want_ctx):
    t = pl.program_id(2)
    tq = q_ref.shape[0]
    n_win = NBR_WIN_ROWS * GRID_W
    n_pairs = q_ref.shape[1] // LANES

    @pl.when(t == 0)
    def _():
        vtc[...] = _transpose_bf16(vc_ref[...])

    def run(with_window):
        if with_window:
            base = jnp.clip(t * NBR_Q_ROWS - NA_KH // 2, 0, grid_rows - NBR_WIN_ROWS)
            off = pl.multiple_of(base * GRID_W, GRID_W)
            vtw = _transpose_bf16(vl_ref[pl.ds(off, n_win), :])
        problems = []
        for u in range(n_pairs):
            ls = slice(u * LANES, (u + 1) * LANES)
            keys, vals, biases = [kc_ref[:, ls]], [vtc[ls, :]], None
            if with_window:
                keys.append(kl_ref[pl.ds(off, n_win), ls])
                vals.append(vtw[ls, :])
                biases = [None, bias_ref[0, u]]
            problems.append((_stack_heads(q_ref[:, ls]), keys, vals, biases))
        for u, o_t in enumerate(_attend_t(problems)):
            o_ref[:, u * LANES:(u + 1) * LANES] = jnp.where(_lo_rows(tq), o_t[:, :tq], o_t[:, tq:]).T.astype(BF16)

    latent_queries = functools.partial(run, True)
    context_queries = functools.partial(run, False)

    if want_ctx:
        pl.when(t < n_row_blocks)(latent_queries)
        pl.when(t >= n_row_blocks)(context_queries)
    else:
        latent_queries()


def _nbr_bias(rpb):
    kw = NA_KW
    n_h = rpb.shape[0]
    col = jnp.arange(GRID_W)
    col_start = jnp.clip(col - kw // 2, 0, GRID_W - kw)
    in_win = (col[None, :] >= col_start[:, None]) & (col[None, :] < col_start[:, None] + kw)
    dc = jnp.clip(col[None, :] - col[:, None], -(kw - 1), kw - 1) + (NA_KW - 1)
    onehot = (dc[:, :, None] == jnp.arange(2 * NA_KW - 1)[None, None, :]).astype(F32)
    by_col = jnp.einsum("hrc,qkc->hrqk", rpb.astype(F32) * LOG2E, onehot, precision=HIGHEST)
    g = jnp.arange(NBR_Q_ROWS)[:, None]
    j = jnp.arange(NBR_WIN_ROWS)[None, :]
    first_row = jnp.stack([jnp.zeros_like(g), g, jnp.full_like(g, NBR_Q_ROWS)])
    rel_off = jnp.array([NA_KH - 1, NA_KH - 1 - NBR_Q_ROWS, NA_KH - 1 - 2 * NBR_Q_ROWS])[:, None, None]
    dr = j[None] - g[None] + rel_off
    row_ok = (j[None] >= first_row) & (j[None] < first_row + NA_KH)
    picked = jnp.take(by_col, jnp.clip(dr, 0, 2 * NA_KH - 2).reshape(-1), axis=1)
    picked = picked.reshape(n_h, 3, NBR_Q_ROWS, NBR_WIN_ROWS, GRID_W, GRID_W)
    ok = row_ok[None, :, :, :, None, None] & in_win[None, None, None, None, :, :]
    bias = jnp.where(ok, picked, MASKED).reshape(n_h // 2, 2, 3, NBR_Q_ROWS, NBR_WIN_ROWS, GRID_W, GRID_W)
    bias = bias.transpose(2, 0, 4, 6, 1, 3, 5)
    return bias.reshape(3, n_h // 2, NBR_WIN_ROWS * GRID_W, 2 * NBR_Q_ROWS * GRID_W).astype(BF16)


def _nbr_attention(rows, want_ctx, misc, bias, q_blk0, k_blk0, v_blk0):
    B, S, C = rows.B, rows.S, rows.C
    tq = NBR_Q_ROWS * GRID_W
    grid_rows = S // GRID_W
    assert grid_rows >= NBR_WIN_ROWS and grid_rows % NBR_Q_ROWS == 0 and C % tq == 0
    qrow, nq, nc = _q_row_map(rows, tq, want_ctx)
    n_q_tiles = nq + (nc if want_ctx else 0)
    ctx_blk0 = rows.n_lat // C
    n_groups = B_HEADS // 2
    n_out_rows = rows.n_all if want_ctx else rows.n_lat
    n_keys = NBR_WIN_ROWS * GRID_W

    width = n_groups * LANES
    assert q_blk0 % n_groups == 0 and k_blk0 % n_groups == 0 and v_blk0 % n_groups == 0
    qb, kb, vb = q_blk0 // n_groups, k_blk0 // n_groups, v_blk0 // n_groups

    def bias_map(g, b, t):
        return (jnp.where(t == 0, 0, jnp.where(t < nq - 1, 1, 2)), 0, 0, 0)

    return pl.pallas_call(
        functools.partial(_nbr_kernel, n_row_blocks=nq, grid_rows=grid_rows, want_ctx=want_ctx),
        grid=(1, B, n_q_tiles),
        in_specs=[pl.BlockSpec((tq, width), lambda g, b, t: (qrow(b, t), qb)),
                  pl.BlockSpec((C, width), lambda g, b, t: (ctx_blk0 + b, kb)),
                  pl.BlockSpec((S, width), lambda g, b, t: (b, kb)),
                  pl.BlockSpec((C, width), lambda g, b, t: (ctx_blk0 + b, vb)),
                  pl.BlockSpec((S, width), lambda g, b, t: (b, vb)),
                  pl.BlockSpec((1, n_groups, n_keys, 2 * tq), bias_map)],
        out_specs=pl.BlockSpec((tq, width), lambda g, b, t: (qrow(b, t), 0)),
        out_shape=jax.ShapeDtypeStruct((n_out_rows, width), BF16),
        scratch_shapes=[pltpu.VMEM((width, C), BF16)],
        compiler_params=_params(("parallel", "parallel", "arbitrary")),
        name="attn_nbr",
    )(misc, misc, misc, misc, misc, bias)


def _residual_epilogue(x, y, gate, g_post, nxt):
    x_new = x + gate * _rms(y, g_post)
    h = None
    if nxt is not None:
        g_pre, sc, sh = nxt
        h = _modulated_norm(x_new, g_pre, sc, sh)
    return x_new, h


def _dot_split(a, b):
    a_hi, b_hi = a.astype(BF16), b.astype(BF16)
    a_lo = (a - a_hi.astype(F32)).astype(BF16)
    b_lo = (b - b_hi.astype(F32)).astype(BF16)
    dot = functools.partial(jnp.dot, preferred_element_type=F32)
    return dot(a_hi, b_hi) + (dot(a_hi, b_lo) + dot(a_lo, b_hi))


def _route(h, w_router):
    logits = _dot_split(h, w_router)
    n_e = logits.shape[1]
    idx = lax.broadcasted_iota(jnp.int32, logits.shape, 1)
    top1 = logits.max(axis=-1, keepdims=True)
    i1 = jnp.where(logits == top1, idx, n_e).min(axis=-1, keepdims=True)
    rest = jnp.where(idx == i1, -jnp.inf, logits)
    top2 = rest.max(axis=-1, keepdims=True)
    i2 = jnp.where(rest == top2, idx, n_e).min(axis=-1, keepdims=True)
    e2 = jnp.exp(top2 - top1)
    w1 = 1.0 / (1.0 + e2)
    w2 = e2 / (1.0 + e2)
    k = lax.broadcasted_iota(jnp.int32, (logits.shape[0], 2), 1)
    return jnp.where(k == 0, i1, i2), jnp.where(k == 0, w1, w2)


def _merge_kernel(gates_ref, oa_ref, ob_ref, oc_ref, od_ref, wbr_ref, wout_ref, x_ref, gate_ref, gpost_ref,
                  gpre_ref, sc_ref, sh_ref, *rest, routed):
    if routed:
        wr_ref, x_out, h_out, idx_out, wts_out = rest
    else:
        x_out, h_out = rest
    tm = x_ref.shape[0]
    for r0 in range(0, tm, tm // MERGE_SPLIT):
        rs = slice(r0, r0 + tm // MERGE_SPLIT)
        m = None
        for j, o_ref in enumerate((oa_ref, ob_ref, oc_ref, od_ref)):
            br = jnp.dot(o_ref[rs, :], wbr_ref[j], preferred_element_type=F32)
            term = gates_ref[rs, j * D_MODEL:(j + 1) * D_MODEL].astype(F32) * br
            m = term if m is None else m + term
        y = jnp.dot(m.astype(BF16), wout_ref[...], preferred_element_type=F32)
        x_new, h = _residual_epilogue(x_ref[rs, :], y, gate_ref[0], gpost_ref[...],
                                      (gpre_ref[...], sc_ref[0], sh_ref[0]))
        if routed:
            idx_out[rs, :], wts_out[rs, :] = _route(h, wr_ref[...])
        x_out[rs, :] = x_new
        h_out[rs, :] = h.astype(h_out.dtype)


def _merge(rows, n_rows, gates, o_all, wbr, wout, x, mods, layer, g_post, g_pre, w_router, tm):
    row = lambda i: (i, 0)
    const2 = lambda i: (0, 0)
    routed = w_router is not None
    in_specs = [pl.BlockSpec((tm, N_BRANCH * D_MODEL), row)]
    in_specs += [pl.BlockSpec((tm, o.shape[1]), row) for o in o_all]
    in_specs += [pl.BlockSpec(wbr.shape, lambda i: (0, 0, 0)), pl.BlockSpec(wout.shape, const2),
                 pl.BlockSpec((tm, D_MODEL), row), rows.mod_spec(layer, 2, tm),
                 pl.BlockSpec((1, D_MODEL), const2), pl.BlockSpec((1, D_MODEL), const2),
                 rows.mod_spec(layer, 4, tm), rows.mod_spec(layer, 3, tm)]
    args = [gates, *o_all, wbr, wout, x, mods, g_post.reshape(1, -1), g_pre.reshape(1, -1), mods, mods]
    out_specs = [pl.BlockSpec((tm, D_MODEL), row), pl.BlockSpec((tm, D_MODEL), row)]
    out_shape = [jax.ShapeDtypeStruct((n_rows, D_MODEL), F32),
                 jax.ShapeDtypeStruct((n_rows, D_MODEL), F32 if routed else BF16)]
    if routed:
        in_specs.append(pl.BlockSpec(w_router.shape, const2))
        args.append(w_router)
        out_specs += [pl.BlockSpec((tm, 2), row), pl.BlockSpec((tm, 2), row)]
        out_shape += [jax.ShapeDtypeStruct((n_rows, 2), jnp.int32), jax.ShapeDtypeStruct((n_rows, 2), F32)]
    return pl.pallas_call(
        functools.partial(_merge_kernel, routed=routed),
        grid=(n_rows // tm,),
        in_specs=in_specs, out_specs=out_specs, out_shape=out_shape,
        compiler_params=_params(("parallel",)),
        name="merge_out",
    )(*args)


FFN_CHUNK = 256


def _ffn_kernel(h_ref, w1_ref, w3_ref, w2_ref, x_ref, gate_ref, gpost_ref, *rest, has_next):
    if has_next:
        gpre_ref, sc_ref, sh_ref, x_out, h_out = rest
    else:
        (x_out,) = rest
    h = h_ref[...]
    y = None
    for c0 in range(0, w1_ref.shape[1], FFN_CHUNK):
        cs = slice(c0, c0 + FFN_CHUNK)
        a = jnp.dot(h, w1_ref[:, cs], preferred_element_type=F32)
        b = jnp.dot(h, w3_ref[:, cs], preferred_element_type=F32)
        part = jnp.dot((a * jax.nn.sigmoid(a) * b).astype(BF16), w2_ref[cs, :], preferred_element_type=F32)
        y = part if y is None else y + part
    nxt = (gpre_ref[...], sc_ref[0], sh_ref[0]) if has_next else None
    x_new, hn = _residual_epilogue(x_ref[...], y, gate_ref[0], gpost_ref[...], nxt)
    x_out[...] = x_new
    if has_next:
        h_out[...] = hn.astype(BF16)


def _ffn(rows, n_rows, h, w1, w3, w2, x, mods, layer, g_post, g_pre_next, tm):
    row = lambda i: (i, 0)
    const2 = lambda i: (0, 0)
    has_next = g_pre_next is not None
    assert w1.shape[1] % FFN_CHUNK == 0
    resident = lambda a: pl.BlockSpec(a.shape, const2, pipeline_mode=pl.Buffered(1))
    in_specs = [pl.BlockSpec((tm, D_MODEL), row), resident(w1), resident(w3), resident(w2),
                pl.BlockSpec((tm, D_MODEL), row), rows.mod_spec(layer, 5, tm), pl.BlockSpec((1, D_MODEL), const2)]
    args = [h, w1, w3, w2, x, mods, g_post.reshape(1, -1)]
    out_specs = [pl.BlockSpec((tm, D_MODEL), row)]
    out_shape = [jax.ShapeDtypeStruct((n_rows, D_MODEL), F32)]
    if has_next:
        in_specs += [pl.BlockSpec((1, D_MODEL), const2), rows.mod_spec(layer + 1, 1, tm),
                     rows.mod_spec(layer + 1, 0, tm)]
        args += [g_pre_next.reshape(1, -1), mods, mods]
        out_specs.append(pl.BlockSpec((tm, D_MODEL), row))
        out_shape.append(jax.ShapeDtypeStruct((n_rows, D_MODEL), BF16))
    return pl.pallas_call(
        functools.partial(_ffn_kernel, has_next=has_next),
        grid=(n_rows // tm,),
        in_specs=in_specs, out_specs=out_specs, out_shape=out_shape,
        compiler_params=_params(("parallel",)),
        name="ffn_dense",
    )(*args)


def _expert_slots(idx, n_tok, tme):
    n_e = N_EXPERTS
    n_asg = 2 * n_tok
    e_flat = idx.reshape(-1)
    onehot = (e_flat[:, None] == jnp.arange(n_e, dtype=jnp.int32)[None, :]).astype(jnp.int32)
    csum = jnp.cumsum(onehot, axis=0)
    counts = csum[-1]
    group = ((counts + tme - 1) // tme) * tme
    group_end = jnp.cumsum(group)
    group_start = group_end - group
    slot = jnp.sum(onehot * (csum - 1 + group_start[None, :]), axis=1)
    n_tiles = n_asg // tme + n_e
    token_of_slot = jnp.zeros((n_tiles * tme,), jnp.int32).at[slot].set(
        jnp.arange(n_asg, dtype=jnp.int32) // 2, unique_indices=True)
    tile_start = jnp.arange(n_tiles, dtype=jnp.int32) * tme
    tile_e = jnp.minimum(jnp.sum((tile_start[:, None] >= group_end[None, :]).astype(jnp.int32), axis=1), n_e - 1)
    sel = (tile_e[:, None] == jnp.arange(n_e, dtype=jnp.int32)[None, :]).astype(jnp.int32)
    filled_end = jnp.sum(sel * (group_start + counts)[None, :], axis=1)
    tile_rows = jnp.clip(filled_end - tile_start, 0, tme)
    return slot.astype(jnp.int32), token_of_slot, tile_e.astype(jnp.int32), tile_rows.astype(jnp.int32)


GATHER_BLOCK = 256
GATHER_UNROLL = 8


def _row_gather(src_hbm, dst, sem, index_of_row, n_blocks):
    def block(blk, carry):
        def body(i, c):
            r = blk * GATHER_BLOCK + i
            src_row = 0 if index_of_row is None else index_of_row(r)
            cp = pltpu.make_async_copy(src_hbm.at[pl.ds(src_row, 1), :], dst.at[pl.ds(r, 1), :], sem)
            if index_of_row is None:
                cp.wait()
            else:
                cp.start()
            return c

        return lax.fori_loop(0, GATHER_BLOCK, body, carry, unroll=GATHER_UNROLL)

    lax.fori_loop(0, n_blocks, block, 0)


def _moe_ffn_kernel(tile_e, tile_rows, tok, h_hbm, w1_ref, w3_ref, w2_ref, y_ref, xg, xb, acc, sem, *, tme, sub):
    t = pl.program_id(0)
    f = pl.program_id(1)
    n_t = pl.num_programs(0)
    buf = t % 2

    def gather_blocks(tile):
        return (tile_rows[tile] + sub - 1) // sub

    def start(tile, b):
        _row_gather(h_hbm, xg.at[b], sem.at[b], lambda r: tok[tile * tme + r], gather_blocks(tile))

    @pl.when(f == 0)
    def _():
        @pl.when(t == 0)
        def _():
            start(0, 0)

        @pl.when(t + 1 < n_t)
        def _():
            start(t + 1, 1 - buf)

        _row_gather(h_hbm, xg.at[buf], sem.at[buf], None, gather_blocks(t))
        acc[...] = jnp.zeros_like(acc)

    def swiglu_rows(rs):
        @pl.when(f == 0)
        def _():
            xb[rs, :] = xg[buf, rs, :].astype(BF16)

        x = xb[rs, :]
        a = jnp.dot(x, w1_ref[0].astype(BF16), preferred_element_type=F32)
        b = jnp.dot(x, w3_ref[0].astype(BF16), preferred_element_type=F32)
        acc[rs, :] += jnp.dot((a * jax.nn.sigmoid(a) * b).astype(BF16), w2_ref[0].astype(BF16),
                              preferred_element_type=F32)

    full = tile_rows[t] == tme

    @pl.when(full)
    def _():
        swiglu_rows(slice(0, tme))

    for sb in range(tme // sub):
        @pl.when(jnp.logical_and(jnp.logical_not(full), sb * sub < tile_rows[t]))
        def _():
            swiglu_rows(slice(sb * sub, (sb + 1) * sub))

    @pl.when(f == pl.num_programs(1) - 1)
    def _():
        y_ref[...] = acc[...]


def _moe_ffn(h, token_of_slot, tile_e, tile_rows, w1, w3, w2, tme, tf):
    n_e, d, ff = w1.shape
    n_tiles = tile_e.shape[0]
    sub = GATHER_BLOCK
    grid_spec = pltpu.PrefetchScalarGridSpec(
        num_scalar_prefetch=3,
        grid=(n_tiles, ff // tf),
        in_specs=[pl.BlockSpec(memory_space=pl.ANY),
                  pl.BlockSpec((1, d, tf), lambda t, f, te, tr, tok: (te[t], 0, f)),
                  pl.BlockSpec((1, d, tf), lambda t, f, te, tr, tok: (te[t], 0, f)),
                  pl.BlockSpec((1, tf, d), lambda t, f, te, tr, tok: (te[t], f, 0))],
        out_specs=pl.BlockSpec((tme, d), lambda t, f, te, tr, tok: (t, 0)),
        scratch_shapes=[pltpu.VMEM((2, tme, d), F32), pltpu.VMEM((tme, d), BF16), pltpu.VMEM((tme, d), F32),
                        pltpu.SemaphoreType.DMA((2,))],
    )
    return pl.pallas_call(
        functools.partial(_moe_ffn_kernel, tme=tme, sub=sub),
        grid_spec=grid_spec,
        out_shape=jax.ShapeDtypeStruct((n_tiles * tme, d), F32),
        compiler_params=_params(("arbitrary", "arbitrary")),
        name="ffn_moe",
    )(tile_e, tile_rows, token_of_slot, h, w1, w3, w2)


def _moe_combine_kernel(slot, y_hbm, wts_ref, x_ref, gate_ref, gpost_ref, *rest, tm, has_next):
    if has_next:
        gpre_ref, sc_ref, sh_ref, x_out, h_out, yb, sem = rest
    else:
        x_out, yb, sem = rest
    i = pl.program_id(0)
    n_i = pl.num_programs(0)
    buf = i % 2

    def start(tile, b):
        for k in range(2):
            _row_gather(y_hbm, yb.at[b, k], sem.at[b], lambda r, k=k: slot[2 * (tile * tm + r) + k], tm // GATHER_BLOCK)

    @pl.when(i == 0)
    def _():
        start(0, 0)

    @pl.when(i + 1 < n_i)
    def _():
        start(i + 1, 1 - buf)

    for k in range(2):
        _row_gather(y_hbm, yb.at[buf, k], sem.at[buf], None, tm // GATHER_BLOCK)
    wts = wts_ref[...]
    y = wts[:, 0:1] * yb[buf, 0] + wts[:, 1:2] * yb[buf, 1]
    nxt = (gpre_ref[...], sc_ref[0], sh_ref[0]) if has_next else None
    x_new, hn = _residual_epilogue(x_ref[...], y, gate_ref[0], gpost_ref[...], nxt)
    x_out[...] = x_new
    if has_next:
        h_out[...] = hn.astype(BF16)


def _moe_combine(rows, n_rows, slot, y, wts, x, mods, layer, g_post, g_pre_next, tm):
    has_next = g_pre_next is not None
    row = lambda i, s: (i, 0)
    const2 = lambda i, s: (0, 0)

    mod_spec = lambda lyr, k: rows.mod_spec(lyr, k, tm)
    in_specs = [pl.BlockSpec(memory_space=pl.ANY), pl.BlockSpec((tm, 2), row),
                pl.BlockSpec((tm, D_MODEL), row), mod_spec(layer, 5), pl.BlockSpec((1, D_MODEL), const2)]
    args = [y, wts, x, mods, g_post.reshape(1, -1)]
    out_specs = [pl.BlockSpec((tm, D_MODEL), row)]
    out_shape = [jax.ShapeDtypeStruct((n_rows, D_MODEL), F32)]
    if has_next:
        in_specs += [pl.BlockSpec((1, D_MODEL), const2), mod_spec(layer + 1, 1), mod_spec(layer + 1, 0)]
        args += [g_pre_next.reshape(1, -1), mods, mods]
        out_specs.append(pl.BlockSpec((tm, D_MODEL), row))
        out_shape.append(jax.ShapeDtypeStruct((n_rows, D_MODEL), BF16))
    grid_spec = pltpu.PrefetchScalarGridSpec(
        num_scalar_prefetch=1, grid=(n_rows // tm,), in_specs=in_specs, out_specs=out_specs,
        scratch_shapes=[pltpu.VMEM((2, 2, tm, D_MODEL), F32), pltpu.SemaphoreType.DMA((2,))])
    return pl.pallas_call(
        functools.partial(_moe_combine_kernel, tm=tm, has_next=has_next),
        grid_spec=grid_spec, out_shape=out_shape,
        compiler_params=_params(("arbitrary",)),
        name="moe_combine",
    )(slot, *args)


def _in_proj_weights(w_in):
    widths = (512, 512, 512, 512, 512, 512, 512, 128, 128, MLA_Q_LORA, MLA_KV_LORA, MLA_ROPE, N_BRANCH * D_MODEL)
    cuts, o = [], 0
    for w in widths:
        cuts.append((o, o + w))
        o += w
    aq, ak, av, bq, bk, bv, cq, ck, cv, dqa, dkva, dkr, gates = [w_in[:, a:b] for a, b in cuts]
    cq = cq.reshape(-1, 2, 4, C_HEAD_DIM).transpose(0, 2, 1, 3).reshape(-1, C_HEADS * C_HEAD_DIM)
    dkr_pad = jnp.zeros((w_in.shape[0], LANES), w_in.dtype).at[:, MLA_NOPE:MLA_NOPE + MLA_ROPE].set(dkr)
    w1 = jnp.concatenate([aq * (A_QK_DIM ** -0.5 * LOG2E), ak], axis=1)
    w2 = jnp.concatenate([av, bq * (B_HEAD_DIM ** -0.5 * LOG2E), bk, bv], axis=1)
    w3 = jnp.concatenate([cq, ck, cv], axis=1)
    w5 = jnp.concatenate([dqa, dkva, dkr_pad], axis=1)
    return [w.astype(BF16) for w in (w1, w2, w3, gates, w5)]


def _mla_weights(w_uq, w_ukv):
    dq = MLA_NOPE + MLA_ROPE
    wuq = jnp.pad(w_uq.reshape(-1, D_HEADS, dq), ((0, 0), (0, 0), (0, LANES - dq))).reshape(-1, D_HEADS * LANES)
    kv = w_ukv.reshape(-1, D_HEADS, MLA_NOPE + MLA_V)
    wuk = jnp.pad(kv[:, :, :MLA_NOPE], ((0, 0), (0, 0), (0, LANES - MLA_NOPE))).reshape(-1, D_HEADS * LANES)
    wuv = kv[:, :, MLA_NOPE:].reshape(-1, D_HEADS * MLA_V)
    return [wuq.astype(BF16), wuk.astype(BF16), wuv.astype(BF16)]


def kernel(x, c, ctx, c_ctx, w_ada, b_ada, g_mix_pre, g_mix_post, g_ffn_pre, g_ffn_post, w_in, lam_q1, lam_k1, lam_q2, lam_k2, g_diff_sub, na_rpb, g_qnorm, g_knorm, g_q_lora, w_uq, g_kv_lora, w_ukv, w_br_a, w_br_b, w_br_c, w_br_d, w_out, w1_dense, w3_dense, w2_dense, w_router, w1_moe, w3_moe, w2_moe):
    B, S, D = x.shape
    C = ctx.shape[1]
    depth = w_in.shape[0]
    assert D == D_MODEL and B + 1 <= MOD_ROWS
    rows = _Rows(B, S, C)
    tq = ATTN_Q_TILE
    tm_proj = _pick_tile((1024, 512, 256), S, B * C)
    tm_row = _pick_tile((512, 256), S, B * C)

    cvec = jnp.zeros((MOD_ROWS, D), F32).at[:B].set(c).at[B].set(c_ctx)
    mods = _mods(cvec, w_ada, b_ada)

    rope_qk = _rope_tables(S, A_QK_DIM, 0, tm_proj)
    rope_mla = _rope_tables(S, MLA_ROPE, MLA_NOPE, tm_proj)

    xs, h = _prenorm(rows, x.reshape(B * S, D), ctx.reshape(B * C, D), g_mix_pre[0], mods, 0, tm_proj)

    for i in range(depth):
        last = i == depth - 1
        want_ctx = not last
        n_rows = rows.n_all if want_ctx else rows.n_lat
        lam_init = 0.8 - 0.6 * math.exp(-0.3 * i)

        gain_c = jnp.concatenate([jnp.tile(g_qnorm[i], C_HEADS) * (C_HEAD_DIM ** -0.5 * LOG2E),
                                  jnp.tile(g_knorm[i], C_KV_HEADS)]).reshape(1, -1).astype(F32)
        proj_weights = _in_proj_weights(w_in[i]) + _mla_weights(w_uq[i], w_ukv[i])
        qka, misc, qkc, gates, qd, kd, vd = _in_proj(rows, h, proj_weights, rope_qk, rope_mla, gain_c,
                                                     g_q_lora[i], g_kv_lora[i], tm_row)

        lam_vecs = [v[i].reshape(1, -1).astype(F32) for v in (lam_q1, lam_k1, lam_q2, lam_k2)]
        oa = _attention(rows, tq, want_ctx, "diff",
                        q_src=(qka, 0), k_src=(qka, A_HEADS), v_src=(misc, 0), n_groups=A_HEADS,
                        extra=lam_vecs + [g_diff_sub[i].reshape(1, -1).astype(F32)], lam_init=lam_init,
                        name="attn_diff")
        ob = _nbr_attention(rows, want_ctx, misc, _nbr_bias(na_rpb[i]), q_blk0=4, k_blk0=8, v_blk0=12)
        oc = _attention(rows, tq, want_ctx, "select",
                        q_src=(qkc, 0), k_src=(qkc, 4), v_src=(qkc, 5), n_groups=C_HEADS // 2, shared_kv=True,
                        name="attn_gqa")
        od = _attention(rows, tq, want_ctx, "select",
                        q_src=(qd, 0), k_src=(kd, 0), v_src=(vd, 0),
                        n_groups=D_HEADS // 2, qk_lanes=2 * LANES, name="attn_mla")

        wbc = w_br_c[i].reshape(2, 4, C_HEAD_DIM, D).transpose(1, 0, 2, 3).reshape(C_HEADS * C_HEAD_DIM, D)
        wbr = jnp.stack([w_br_a[i], w_br_b[i], wbc, w_br_d[i]]).astype(BF16)
        moe = i % 2 == 1
        j = i // 2
        merged = _merge(rows, n_rows, gates, (oa, ob, oc, od), wbr, w_out[i].astype(BF16), xs, mods, i,
                        g_mix_post[i], g_ffn_pre[i], w_router[j] if moe else None, tm_row)
        g_pre_next = None if last else g_mix_pre[i + 1]
        if moe:
            xs, h2, route_idx, route_wts = merged
            slot, token_of_slot, tile_e, tile_rows = _expert_slots(route_idx, n_rows, MOE_TILE)
            y = _moe_ffn(h2, token_of_slot, tile_e, tile_rows, w1_moe[j], w3_moe[j], w2_moe[j], MOE_TILE, MOE_F_TILE)
            outs = _moe_combine(rows, n_rows, slot, y, route_wts, xs, mods, i, g_ffn_post[i], g_pre_next,
                                GATHER_BLOCK)
        else:
            xs, h2 = merged
            outs = _ffn(rows, n_rows, h2, w1_dense[j].astype(BF16), w3_dense[j].astype(BF16),
                        w2_dense[j].astype(BF16), xs, mods, i, g_ffn_post[i], g_pre_next, tm_row)
        if last:
            xs = outs[0]
        else:
            xs, h = outs
    return xs[:B * S].reshape(B, S, D)
```
